```python
import math
import jax, jax.numpy as jnp
from jax import lax
import numpy as np

D_MODEL = 1024
BATCH = 8
SEQ = 4096
DEPTH = 2

GRID_W = 64
CTX_LEN = 256
EPS = 1e-6
ROPE_BASE = 10000.0

ATT_HEADS = 4
ATT_QK = 64
ATT_V = 2 * ATT_QK
ATT_W = ATT_HEADS * ATT_V
ATT_BLOCK = 128

RET_HEADS = 4
RET_QK = 64
RET_V = 64
RET_W = RET_HEADS * RET_V
RET_CHUNK = 128

LRU_W = 256
LRU_BLOCKS = 4
LRU_BW = LRU_W // LRU_BLOCKS
LRU_CONV = 4
LRU_C = 8.0

MIX_W = ATT_W + RET_W + LRU_W
IN_SPLITS = (ATT_HEADS * 2 * ATT_QK, ATT_HEADS * 2 * ATT_QK, ATT_W,
             RET_HEADS * RET_QK, RET_HEADS * RET_QK, RET_W, RET_W, LRU_W, LRU_W)
IN_COLS = sum(IN_SPLITS)

N_EXPERTS = 16
EC_CAPACITY = 2
MOE_FF = 1024

kernel_name = 'hybrid_diffusion_diffattn_retention_rglru_ecmoe'


def _rms(x, g):
    xf = x.astype(jnp.float32)
    y = xf * lax.rsqrt(jnp.mean(xf * xf, axis=-1, keepdims=True) + EPS)
    return (y * g.astype(jnp.float32)).astype(x.dtype)


def _modulate(h, shift, scale):
    return h * (1 + scale) + shift


def _rope_tables(n_tokens):
    rows = n_tokens // GRID_W
    row = jnp.repeat(jnp.arange(rows), GRID_W).astype(jnp.float32)
    col = jnp.tile(jnp.arange(GRID_W), rows).astype(jnp.float32)
    axis_dim = ATT_QK // 2
    inv = 1.0 / (ROPE_BASE ** (jnp.arange(0, axis_dim, 2, dtype=jnp.float32) / axis_dim))
    ang = jnp.concatenate([row[:, None] * inv, col[:, None] * inv], axis=-1)
    return jnp.cos(ang), jnp.sin(ang)


def _rotate(t, cos, sin):
    n = t.shape[-1] // 2
    t1, t2 = t[..., :n], t[..., n:]
    return jnp.concatenate([t1 * cos - t2 * sin, t1 * sin + t2 * cos], axis=-1)


def _axial_rope(x, cos, sin):
    a = ATT_QK // 2
    f = a // 2
    out = jnp.concatenate([_rotate(x[..., :a], cos[:, :f], sin[:, :f]),
                           _rotate(x[..., a:], cos[:, f:], sin[:, f:])], axis=-1)
    return out.astype(x.dtype)


def _diff_attn_block(q, k, v, lam):
    s = jnp.einsum('bhmqd,bhmkd->bhmqk', q, k).astype(jnp.float32) * (ATT_QK ** -0.5)
    p = jax.nn.softmax(s, axis=-1)
    a = p[:, :, 0] - lam * p[:, :, 1]
    return jnp.einsum('bhqk,bhkv->bhqv', a, v.astype(jnp.float32))


def _diff_attn_latent(q, k, v, lam):
    b, h, _, n, d = q.shape
    nb = n // ATT_BLOCK
    qb = jnp.moveaxis(q.reshape(b, h, 2, nb, ATT_BLOCK, d), 3, 0)
    ob = lax.map(lambda qq: _diff_attn_block(qq, k, v, lam), qb)
    return jnp.moveaxis(ob, 0, 2).reshape(b, h, n, ATT_V)


def _retention_scan(q, k, v, log_g, s0, inclusive):
    b, h, n_tok, dk = q.shape
    dv = v.shape[-1]
    nc = n_tok // RET_CHUNK
    idx = jnp.arange(RET_CHUNK, dtype=jnp.float32)
    diff = idx[:, None] - idx[None, :]
    mask = (diff >= 0) if inclusive else (diff > 0)
    decay_mat = jnp.where(mask, jnp.exp(log_g[:, None, None] * jnp.where(mask, diff, 0.0)), 0.0)
    cross = jnp.exp(log_g[:, None] * (idx + 1.0))[..., None]
    inner = jnp.exp(log_g[:, None] * (RET_CHUNK - 1.0 - idx))[..., None]
    chunk_decay = jnp.exp(log_g * RET_CHUNK)[:, None, None]
    qs = jnp.moveaxis(q.reshape(b, h, nc, RET_CHUNK, dk), 2, 0)
    ks = jnp.moveaxis(k.reshape(b, h, nc, RET_CHUNK, dk), 2, 0)
    vs = jnp.moveaxis(v.reshape(b, h, nc, RET_CHUNK, dv), 2, 0)

    def step(s, inp):
        qc, kc, vc = inp
        sc = jnp.einsum('bhnd,bhmd->bhnm', qc, kc) * decay_mat
        o = jnp.einsum('bhnm,bhmv->bhnv', sc, vc) + jnp.einsum('bhnd,bhdv->bhnv', qc, s) * cross
        s = s * chunk_decay + jnp.einsum('bhmd,bhmv->bhdv', kc * inner, vc)
        return s, o

    s, o = lax.scan(step, s0, (qs, ks, vs))
    return jnp.moveaxis(o, 0, 2).reshape(b, h, n_tok, dv), s


def _retention_state(k, v, log_g):
    n_tok = k.shape[2]
    w = jnp.exp(log_g[:, None] * (n_tok - 1.0 - jnp.arange(n_tok, dtype=jnp.float32)))
    return jnp.einsum('bhld,hl,bhlv->bhdv', k, w, v)


def _short_conv(t, w, bias):
    y = lax.conv_general_dilated(t, w[:, None, :].astype(t.dtype), window_strides=(1,),
                                 padding=[(LRU_CONV // 2, LRU_CONV - 1 - LRU_CONV // 2)],
                                 dimension_numbers=('NWC', 'WIO', 'NWC'),
                                 feature_group_count=LRU_W)
    return y + bias


def _lin_combine(left, right):
    a1, b1 = left
    a2, b2 = right
    return a1 * a2, a2 * b1 + b2


def _rglru(x, gate_w, gate_b, lam, h0):
    x = x.astype(jnp.float32)
    b, n_tok, _ = x.shape
    xb = x.reshape(b, n_tok, LRU_BLOCKS, LRU_BW)
    gates = jnp.einsum('blnc,gncd->gblnd', xb, gate_w.astype(jnp.float32)).reshape(2, b, n_tok, LRU_W)
    gates = gates + gate_b.astype(jnp.float32)[:, None, None, :]
    r = jax.nn.sigmoid(gates[0])
    i = jax.nn.sigmoid(gates[1])
    log_a = -LRU_C * jax.nn.softplus(-lam.astype(jnp.float32)) * r
    a = jnp.exp(log_a)
    u = jnp.sqrt(-jnp.expm1(2.0 * log_a)) * (i * x)
    u = u.at[:, 0].add(a[:, 0] * h0)
    _, h = lax.associative_scan(_lin_combine, (a, u), axis=1)
    return h


def _mixer(hl, hc, cos, sin, layer_idx, need_ctx, w_in, w_out, q_norm_g, k_norm_g, lam_p, subln_g,
           ret_log_decay, ret_norm_g, conv_w, conv_b, gate_w, gate_b, lru_lam, lru_norm_g):
    f32 = jnp.float32
    cuts = np.cumsum(IN_SPLITS)[:-1].tolist()
    qa_l, ka_l, va_l, qr_l, kr_l, vr_l, gr_l, xu_l, gu_l = jnp.split(hl @ w_in, cuts, axis=-1)
    qa_c, ka_c, va_c, qr_c, kr_c, vr_c, gr_c, xu_c, gu_c = jnp.split(hc @ w_in, cuts, axis=-1)

    def qk_heads(t, g):
        b_, n_, _ = t.shape
        return _rms(t.reshape(b_, n_, ATT_HEADS, 2, ATT_QK).transpose(0, 2, 3, 1, 4), g)

    def heads(t, hd):
        b_, n_, _ = t.shape
        return t.reshape(b_, n_, -1, hd).transpose(0, 2, 1, 3)

    def merge(o):
        b_, h_, n_, hd = o.shape
        return o.transpose(0, 2, 1, 3).reshape(b_, n_, h_ * hd)

    lam_init = 0.8 - 0.6 * math.exp(-0.3 * layer_idx)
    lp = lam_p.astype(f32)
    lam = jnp.exp(jnp.sum(lp[0] * lp[1])) - jnp.exp(jnp.sum(lp[2] * lp[3])) + lam_init
    q_l = _axial_rope(qk_heads(qa_l, q_norm_g), cos, sin)
    k_l = _axial_rope(qk_heads(ka_l, k_norm_g), cos, sin)
    k_c = qk_heads(ka_c, k_norm_g)
    v_l = heads(va_l, ATT_V)
    v_c = heads(va_c, ATT_V)
    k_all = jnp.concatenate([k_c, k_l], axis=3)
    v_all = jnp.concatenate([v_c, v_l], axis=2)

    def att_out(o):
        return merge(_rms(o, subln_g) * (1.0 - lam_init))

    att_l = att_out(_diff_attn_latent(q_l, k_all, v_all, lam))

    flip = lambda t: t[:, :, ::-1]
    lg_f = ret_log_decay[0].astype(f32)
    lg_b = ret_log_decay[1].astype(f32)
    kscale = RET_QK ** -0.5
    rq_l = heads(qr_l, RET_QK).astype(f32)
    rk_l = (heads(kr_l, RET_QK) * kscale).astype(f32)
    rv_l = heads(vr_l, RET_V).astype(f32)
    rk_c = (heads(kr_c, RET_QK) * kscale).astype(f32)
    rv_c = heads(vr_c, RET_V).astype(f32)
    if need_ctx:
        rq_c = heads(qr_c, RET_QK).astype(f32)
        zero_s = jnp.zeros((hc.shape[0], RET_HEADS, RET_QK, RET_V), f32)
        o_cf, s_cf = _retention_scan(rq_c, rk_c, rv_c, lg_f, zero_s, True)
        o_cb, s_cb = _retention_scan(flip(rq_c), flip(rk_c), flip(rv_c), lg_b, zero_s, False)
    else:
        s_cf = _retention_state(rk_c, rv_c, lg_f)
        s_cb = _retention_state(flip(rk_c), flip(rv_c), lg_b)
    o_lf, _ = _retention_scan(rq_l, rk_l, rv_l, lg_f, s_cf, True)
    o_lb, _ = _retention_scan(flip(rq_l), flip(rk_l), flip(rv_l), lg_b, s_cb, False)

    def ret_out(o, g):
        return merge(_rms(o, ret_norm_g)) * jax.nn.silu(g.astype(f32))

    ret_l = ret_out(o_lf + flip(o_lb), gr_l)

    u_l = _short_conv(xu_l, conv_w, conv_b)
    u_c = _short_conv(xu_c, conv_w, conv_b)
    h0 = jnp.zeros((hc.shape[0], LRU_W), f32)
    lc_f = _rglru(u_c, gate_w[0], gate_b[0], lru_lam[0], h0)
    ll_f = _rglru(u_l, gate_w[0], gate_b[0], lru_lam[0], lc_f[:, -1])
    lc_b = _rglru(u_c[:, ::-1], gate_w[1], gate_b[1], lru_lam[1], h0)
    ll_b = _rglru(u_l[:, ::-1], gate_w[1], gate_b[1], lru_lam[1], lc_b[:, -1])

    def lru_out(hf, hb_rev, g):
        return _rms((hf + hb_rev[:, ::-1]) * jax.nn.gelu(g.astype(f32)), lru_norm_g)

    lru_l = lru_out(ll_f, ll_b, gu_l)

    mix_l = jnp.concatenate([att_l, ret_l, lru_l], axis=-1).astype(hl.dtype) @ w_out
    if not need_ctx:
        return mix_l, None
    att_c = att_out(_diff_attn_block(qk_heads(qa_c, q_norm_g), k_c, v_c, lam))
    ret_c = ret_out(o_cf + flip(o_cb), gr_c)
    lru_c = lru_out(lc_f, lc_b, gu_c)
    mix_c = jnp.concatenate([att_c, ret_c, lru_c], axis=-1).astype(hc.dtype) @ w_out
    return mix_l, mix_c


def _expert_choice_ffn(h, router_w, w_gate, w_up, w_down):
    b, n_tok, _ = h.shape
    cap = EC_CAPACITY * n_tok // N_EXPERTS
    probs = jax.nn.softmax((h @ router_w).astype(jnp.float32), axis=-1)
    aff, idx = lax.top_k(jnp.swapaxes(probs, 1, 2), cap)
    bidx = jnp.arange(b)[:, None, None]
    xs = h[bidx, idx]
    hid = jax.nn.silu(jnp.einsum('becd,edf->becf', xs, w_gate)) * jnp.einsum('becd,edf->becf', xs, w_up)
    out = jnp.einsum('becf,efd->becd', hid, w_down) * aff[..., None]
    return jnp.zeros_like(h).at[bidx, idx].add(out.astype(h.dtype))


def setup_inputs(seed: int = 0) -> dict:
    key = jax.random.key(seed)
    ks = jax.random.split(key, 32)
    f32 = jnp.float32
    D = D_MODEL

    def nrm(k, shape, scale):
        return jax.random.normal(k, shape, f32) * scale

    ret_base = jnp.asarray(np.log(1.0 - 2.0 ** (-5.0 - np.arange(RET_HEADS))), f32)
    u = jax.random.uniform(ks[20], (DEPTH, 2, LRU_W), f32, 0.9, 0.999)
    a0 = u ** (1.0 / LRU_C)
    return {
        'x': nrm(ks[0], (BATCH, SEQ, D), 1.0),
        'c': nrm(ks[1], (BATCH, D), 1.0),
        'ctx': nrm(ks[2], (BATCH, CTX_LEN, D), 1.0),
        'c_ctx': nrm(ks[3], (D,), 1.0),
        'mod_w': nrm(ks[4], (DEPTH, D, 6 * D), 0.5 * D ** -0.5),
        'mod_b': nrm(ks[5], (DEPTH, 6 * D), 0.01),
        'norm1_g': 1.0 + nrm(ks[6], (DEPTH, D), 0.02),
        'norm2_g': 1.0 + nrm(ks[7], (DEPTH, D), 0.02),
        'w_in': nrm(ks[8], (DEPTH, D, IN_COLS), D ** -0.5),
        'w_out': nrm(ks[9], (DEPTH, MIX_W, D), MIX_W ** -0.5),
        'att_q_norm_g': 1.0 + nrm(ks[10], (DEPTH, ATT_QK), 0.02),
        'att_k_norm_g': 1.0 + nrm(ks[11], (DEPTH, ATT_QK), 0.02),
        'att_lambda': nrm(ks[12], (DEPTH, 4, ATT_QK), 0.1),
        'att_subln_g': 1.0 + nrm(ks[13], (DEPTH, ATT_V), 0.02),
        'ret_log_decay': ret_base * jnp.exp(nrm(ks[14], (DEPTH, 2, RET_HEADS), 0.1)),
        'ret_norm_g': 1.0 + nrm(ks[15], (DEPTH, RET_V), 0.02),
        'lru_conv_w': nrm(ks[16], (DEPTH, LRU_CONV, LRU_W), LRU_CONV ** -0.5),
        'lru_conv_b': nrm(ks[17], (DEPTH, LRU_W), 0.01),
        'lru_gate_w': nrm(ks[18], (DEPTH, 2, 2, LRU_BLOCKS, LRU_BW, LRU_BW), LRU_BW ** -0.5),
        'lru_gate_b': nrm(ks[19], (DEPTH, 2, 2, LRU_W), 0.01),
        'lru_lambda': jnp.log(a0) - jnp.log1p(-a0),
        'lru_norm_g': 1.0 + nrm(ks[21], (DEPTH, LRU_W), 0.02),
        'router_w': nrm(ks[22], (DEPTH, D, N_EXPERTS), D ** -0.5),
        'exp_w_gate': nrm(ks[23], (DEPTH, N_EXPERTS, D, MOE_FF), D ** -0.5),
        'exp_w_up': nrm(ks[24], (DEPTH, N_EXPERTS, D, MOE_FF), D ** -0.5),
        'exp_w_down': nrm(ks[25], (DEPTH, N_EXPERTS, MOE_FF, D), MOE_FF ** -0.5),
    }


def reference(x, c, ctx, c_ctx, mod_w, mod_b, norm1_g, norm2_g, w_in, w_out, att_q_norm_g, att_k_norm_g,
              att_lambda, att_subln_g, ret_log_decay, ret_norm_g, lru_conv_w, lru_conv_b, lru_gate_w,
              lru_gate_b, lru_lambda, lru_norm_g, router_w, exp_w_gate, exp_w_up, exp_w_down):
    cos, sin = _rope_tables(x.shape[1])
    silu_c = jax.nn.silu(c)
    silu_cc = jax.nn.silu(c_ctx)
    xl, xc = x, ctx
    for i in range(DEPTH):
        need_ctx = i < DEPTH - 1
        sh1_l, sc1_l, g1_l, sh2_l, sc2_l, g2_l = jnp.split((silu_c @ mod_w[i] + mod_b[i])[:, None, :], 6, axis=-1)
        sh1_c, sc1_c, g1_c, sh2_c, sc2_c, g2_c = jnp.split(silu_cc @ mod_w[i] + mod_b[i], 6, axis=-1)
        hl = _modulate(_rms(xl, norm1_g[i]), sh1_l, sc1_l)
        hc = _modulate(_rms(xc, norm1_g[i]), sh1_c, sc1_c)
        mix_l, mix_c = _mixer(hl, hc, cos, sin, i, need_ctx, w_in[i], w_out[i], att_q_norm_g[i],
                              att_k_norm_g[i], att_lambda[i], att_subln_g[i], ret_log_decay[i], ret_norm_g[i],
                              lru_conv_w[i], lru_conv_b[i], lru_gate_w[i], lru_gate_b[i], lru_lambda[i],
                              lru_norm_g[i])
        xl = xl + g1_l * mix_l
        xl = xl + g2_l * _expert_choice_ffn(_modulate(_rms(xl, norm2_g[i]), sh2_l, sc2_l),
                                            router_w[i], exp_w_gate[i], exp_w_up[i], exp_w_down[i])
        if need_ctx:
            xc = xc + g1_c * mix_c
            xc = xc + g2_c * _expert_choice_ffn(_modulate(_rms(xc, norm2_g[i]), sh2_c, sc2_c),
                                                router_w[i], exp_w_gate[i], exp_w_up[i], exp_w_down[i])
    return xl
```

```python
import functools
import math

import jax
import jax.numpy as jnp
import numpy as np
from jax import lax
from jax.experimental import pallas as pl
from jax.experimental.pallas import tpu as pltpu

F32 = jnp.float32
BF16 = jnp.bfloat16
I32 = jnp.int32

ATT_HEADS = 4
ATT_QK = 64
ATT_V = 2 * ATT_QK
ATT_W = ATT_HEADS * ATT_V
RET_HEADS = 4
RET_QK = 64
RET_W = 256
LRU_W = 256
LRU_C = 8.0
N_EXPERTS = 16
EC_CAPACITY = 2
EPS = 1e-6
ROPE_BASE = 10000.0
GRID_W = 64
IN_COLS = 3072

LANES = 128
TOK_TILE = 256
KEY_BLOCK = 512
CHUNK = 128
SLOT_TILE = 128
TOK_BLOCK = 256
MAX_SLOT_TILES = 8
VMEM_LIMIT = 56 * 1024 * 1024


def _cparams(sem, vmem=None):
    return pltpu.CompilerParams(dimension_semantics=sem, vmem_limit_bytes=vmem)


def _sigmoid(x):
    return 1.0 / (1.0 + jnp.exp(-x))


def _silu(x):
    return x * _sigmoid(x)


def _group_mean_sq(t, bd):
    sq = t * t
    hi = sq.astype(BF16)
    lo = (sq - hi.astype(F32)).astype(BF16)
    ss = jnp.dot(hi, bd, preferred_element_type=F32) + jnp.dot(lo, bd, preferred_element_type=F32)
    return ss * (1.0 / 64.0)


def _mod_kernel(c_ref, w_ref, b_ref, o_ref):
    s = _silu(c_ref[...]).astype(BF16)
    o_ref[...] = jnp.dot(s, w_ref[...].astype(BF16), preferred_element_type=F32) + b_ref[...]


def _modulation(cc, mod_w, mod_b):
    depth, d, d6 = mod_w.shape
    rows = cc.shape[0]
    return pl.pallas_call(
        _mod_kernel,
        grid=(depth, d6 // d),
        in_specs=[pl.BlockSpec((rows, d), lambda l, j: (0, 0)),
                  pl.BlockSpec((None, d, d), lambda l, j: (l, 0, j)),
                  pl.BlockSpec((None, 1, d), lambda l, j: (l, 0, j))],
        out_specs=pl.BlockSpec((None, rows, d), lambda l, j: (l, 0, j)),
        out_shape=jax.ShapeDtypeStruct((depth, rows, d6), F32),
        compiler_params=_cparams(("arbitrary", "arbitrary")),
        name="modulation",
    )(cc, mod_w, mod_b.reshape(depth, 1, d6))


def _inproj_kernel(x_ref, mod_ref, ng_ref, w_ref, rope_ref, gqk_ref, bd_ref,
                   q_ref, kt_ref, v_ref, rq_ref, rk_ref, rv_ref, gr_ref, xu_ref, gu_ref):
    x = x_ref[...]
    ms = jnp.mean(x * x, axis=-1, keepdims=True)
    y = x * lax.rsqrt(ms + EPS) * ng_ref[...]
    h = y * (1.0 + mod_ref[1:2, :]) + mod_ref[0:1, :]
    hb = h.astype(BF16)

    def proj(c0, c1):
        return jnp.dot(hb, w_ref[:, c0:c1], preferred_element_type=F32)

    bd = bd_ref[...]
    cosr = rope_ref[0]
    sin_up = rope_ref[1]
    sin_dn = rope_ref[2]

    def norm_rope(t, g):
        tn = t * lax.rsqrt(_group_mean_sq(t, bd) + EPS) * g
        return tn * cosr + pltpu.roll(tn, LANES - 16, 1) * sin_up + pltpu.roll(tn, 16, 1) * sin_dn

    for hd in range(ATT_HEADS):
        c0 = hd * ATT_V
        tq = proj(c0, c0 + ATT_V)
        q_ref[:, c0:c0 + ATT_V] = (norm_rope(tq, gqk_ref[0:1, :]) * (ATT_QK ** -0.5)).astype(BF16)
        tk = proj(ATT_W + c0, ATT_W + c0 + ATT_V)
        kt_ref[c0:c0 + ATT_V, :] = norm_rope(tk, gqk_ref[1:2, :]).T.astype(BF16)
    v_ref[...] = proj(2 * ATT_W, 3 * ATT_W).astype(BF16)
    base = 3 * ATT_W
    rq_ref[...] = proj(base, base + RET_W)
    rk_ref[...] = proj(base + RET_W, base + 2 * RET_W) * (RET_QK ** -0.5)
    rv_ref[...] = proj(base + 2 * RET_W, base + 3 * RET_W)
    gr_ref[...] = proj(base + 3 * RET_W, base + 4 * RET_W)
    xu_ref[...] = proj(base + 4 * RET_W, base + 4 * RET_W + LRU_W)
    gu_ref[...] = proj(base + 4 * RET_W + LRU_W, base + 4 * RET_W + 2 * LRU_W)


def _inproj(xcat, mod8, norm_g, w_in_b, rope, gqk, bd, n_ctx_tiles):
    b, s, d = xcat.shape
    tm = TOK_TILE
    seg = lambda i: jnp.minimum(i // max(n_ctx_tiles, 1), 1) if n_ctx_tiles else 1
    tok = lambda w: pl.BlockSpec((None, tm, w), lambda bi, i: (bi, i, 0))
    f32o = lambda w: jax.ShapeDtypeStruct((b, s, w), F32)
    return pl.pallas_call(
        _inproj_kernel,
        grid=(b, s // tm),
        in_specs=[tok(d),
                  pl.BlockSpec((None, None, 8, d), lambda bi, i: (bi, seg(i), 0, 0)),
                  pl.BlockSpec((1, d), lambda bi, i: (0, 0)),
                  pl.BlockSpec((d, IN_COLS), lambda bi, i: (0, 0)),
                  pl.BlockSpec((3, tm, LANES), lambda bi, i: (0, i, 0)),
                  pl.BlockSpec((2, LANES), lambda bi, i: (0, 0)),
                  pl.BlockSpec((LANES, LANES), lambda bi, i: (0, 0))],
        out_specs=[tok(ATT_W),
                   pl.BlockSpec((None, ATT_W, tm), lambda bi, i: (bi, 0, i)),
                   tok(ATT_W), tok(RET_W), tok(RET_W), tok(RET_W), tok(RET_W), tok(LRU_W), tok(LRU_W)],
        out_shape=[jax.ShapeDtypeStruct((b, s, ATT_W), BF16),
                   jax.ShapeDtypeStruct((b, ATT_W, s), BF16),
                   jax.ShapeDtypeStruct((b, s, ATT_W), BF16),
                   f32o(RET_W), f32o(RET_W), f32o(RET_W), f32o(RET_W), f32o(LRU_W), f32o(LRU_W)],
        compiler_params=_cparams(("arbitrary", "arbitrary"), VMEM_LIMIT),
        name="inproj",
    )(xcat, mod8, norm_g, w_in_b, rope, gqk, bd)


def _attn_kernel(lam_ref, q_ref, kt_ref, v_ref, g_ref, o_ref, *, ctx_len, q_off, n_ctx_tiles, lam_init):
    seq_len = kt_ref.shape[1]
    lp = lam_ref[...]
    lam = (jnp.exp(jnp.sum(lp[0:1, :] * lp[1:2, :], axis=1, keepdims=True))
           - jnp.exp(jnp.sum(lp[2:3, :] * lp[3:4, :], axis=1, keepdims=True)) + lam_init)
    q = q_ref[...]
    lane = lax.broadcasted_iota(I32, q.shape, 1)

    def attend(n_keys):
        outs = []
        for mi in range(2):
            keep = (lane < ATT_QK) if mi == 0 else (lane >= ATT_QK)
            qm = jnp.where(keep, q, jnp.zeros_like(q))
            s = jnp.dot(qm, kt_ref[:, 0:n_keys], preferred_element_type=F32)
            p = jnp.exp(s - jnp.max(s, axis=1, keepdims=True))
            acc = jnp.dot(p.astype(BF16), v_ref[0:n_keys, :], preferred_element_type=F32)
            outs.append(acc / jnp.sum(p, axis=1, keepdims=True))
        o = outs[0] - lam * outs[1]
        o = o * lax.rsqrt(jnp.mean(o * o, axis=-1, keepdims=True) + EPS) * g_ref[...] * (1.0 - lam_init)
        o_ref[...] = o.astype(BF16)

    if q_off >= n_ctx_tiles:
        attend(seq_len)
    else:
        is_ctx = pl.program_id(2) + q_off < n_ctx_tiles
        pl.when(is_ctx)(lambda: attend(ctx_len))
        pl.when(jnp.logical_not(is_ctx))(lambda: attend(seq_len))


def _attention(lam_p, q, kt, v, subln_g, ctx_len, q_off, n_ctx_tiles, lam_init):
    b, s, _ = q.shape
    tq = TOK_TILE
    nq = s // tq - q_off
    kern = functools.partial(_attn_kernel, ctx_len=ctx_len, q_off=q_off, n_ctx_tiles=n_ctx_tiles,
                             lam_init=lam_init)
    return pl.pallas_call(
        kern,
        grid=(b, ATT_HEADS, nq),
        in_specs=[pl.BlockSpec((4, LANES), lambda bi, h, i: (0, 0)),
                  pl.BlockSpec((None, tq, ATT_V), lambda bi, h, i: (bi, i + q_off, h)),
                  pl.BlockSpec((None, ATT_V, s), lambda bi, h, i: (bi, h, 0)),
                  pl.BlockSpec((None, s, ATT_V), lambda bi, h, i: (bi, 0, h)),
                  pl.BlockSpec((1, ATT_V), lambda bi, h, i: (0, 0))],
        out_specs=pl.BlockSpec((None, tq, ATT_V), lambda bi, h, i: (bi, i, h)),
        out_shape=jax.ShapeDtypeStruct((b, nq * tq, ATT_W), BF16),
        compiler_params=_cparams(("arbitrary", "arbitrary", "arbitrary"), VMEM_LIMIT),
        name="diff_attention",
    )(lam_p, q, kt, v, subln_g)


def _ret_kernel(lg_ref, q_ref, k_ref, v_ref, gr_ref, g_ref, bd_ref, o_ref,
                of_ref, ob_ref, st_ref, dm_ref, cr_ref, in_ref, cd_ref, *, n_ctx_chunks, n_chunks, out_chunk0):
    c = CHUNK
    rown = lax.broadcasted_iota(I32, (c, c), 0)
    colm = lax.broadcasted_iota(I32, (c, c), 1)
    low = colm < RET_QK
    rowf = rown.astype(F32)
    for d in range(2):
        if d == 0:
            diff = (rown - colm).astype(F32)
            mask = rown >= colm
            cross_pw = rowf + 1.0
            inner_pw = (c - 1.0) - rowf
        else:
            diff = (colm - rown).astype(F32)
            mask = colm > rown
            cross_pw = c - rowf
            inner_pw = rowf
        for hd in range(RET_HEADS):
            lg = lg_ref[d, hd]
            dm_ref[d, hd] = jnp.where(mask, jnp.exp(lg * jnp.where(mask, diff, 0.0)), 0.0)
        for hp in range(2):
            lg_lane = jnp.where(low, lg_ref[d, 2 * hp], lg_ref[d, 2 * hp + 1])
            cr_ref[d, hp] = jnp.exp(lg_lane * cross_pw)
            in_ref[d, hp] = jnp.exp(lg_lane * inner_pw)
            cd_ref[d, hp] = jnp.exp(lg_lane * float(c))
    st_ref[...] = jnp.zeros(st_ref.shape, F32)
    bdmask = (rown < RET_QK) == (colm < RET_QK)

    def step(i, carry):
        for d in range(2):
            if d == 0:
                ci = i
            else:
                ci = jnp.where(i < n_ctx_chunks, n_ctx_chunks - 1 - i, n_chunks - 1 - (i - n_ctx_chunks))
            r0 = pl.multiple_of(ci * c, c)
            dst = of_ref if d == 0 else ob_ref
            for hp in range(2):
                cols = slice(hp * LANES, (hp + 1) * LANES)
                kf = k_ref[pl.ds(r0, c), cols]
                qb = q_ref[pl.ds(r0, c), cols].astype(BF16)
                kb = kf.astype(BF16)
                vb = v_ref[pl.ds(r0, c), cols].astype(BF16)
                zero = jnp.zeros_like(qb)
                nt = (((1,), (1,)), ((), ()))
                s_a = lax.dot_general(jnp.where(low, qb, zero), kb, nt, preferred_element_type=F32)
                s_b = lax.dot_general(jnp.where(low, zero, qb), kb, nt, preferred_element_type=F32)
                sc = jnp.concatenate([s_a * dm_ref[d, 2 * hp], s_b * dm_ref[d, 2 * hp + 1]], axis=1).astype(BF16)
                vbd = jnp.concatenate([jnp.where(low, vb, zero), jnp.where(low, zero, vb)], axis=0)
                intra = jnp.dot(sc, vbd, preferred_element_type=F32)
                st = st_ref[d, hp]
                inter = jnp.dot(qb, st.astype(BF16), preferred_element_type=F32) * cr_ref[d, hp]
                dst[pl.ds(r0, c), cols] = intra + inter
                kin = (kf * in_ref[d, hp]).astype(BF16)
                kv = lax.dot_general(kin, vb, (((0,), (0,)), ((), ())), preferred_element_type=F32)
                st_ref[d, hp] = st * cd_ref[d, hp] + jnp.where(bdmask, kv, 0.0)
        return carry

    lax.fori_loop(0, n_chunks, step, 0)

    bd = bd_ref[...]

    def finish(i, carry):
        r0 = pl.multiple_of((i + out_chunk0) * c, c)
        ro = pl.multiple_of(i * c, c)
        for hp in range(2):
            cols = slice(hp * LANES, (hp + 1) * LANES)
            o = of_ref[pl.ds(r0, c), cols] + ob_ref[pl.ds(r0, c), cols]
            on = o * lax.rsqrt(_group_mean_sq(o, bd) + EPS) * g_ref[:, cols]
            o_ref[pl.ds(ro, c), cols] = (on * _silu(gr_ref[pl.ds(r0, c), cols])).astype(BF16)
        return carry

    lax.fori_loop(0, n_chunks - out_chunk0, finish, 0)


def _retention(log_decay, rq, rk, rv, gr, norm_g, bd, ctx_len, out_off):
    b, s, w = rq.shape
    n_chunks = s // CHUNK
    out_chunk0 = out_off // CHUNK
    kern = functools.partial(_ret_kernel, n_ctx_chunks=ctx_len // CHUNK, n_chunks=n_chunks, out_chunk0=out_chunk0)
    seq = pl.BlockSpec((None, s, w), lambda bi: (bi, 0, 0))
    tab = pltpu.VMEM((2, 2, CHUNK, CHUNK), F32)
    return pl.pallas_call(
        kern,
        grid=(b,),
        in_specs=[pl.BlockSpec(memory_space=pltpu.SMEM), seq, seq, seq, seq,
                  pl.BlockSpec((1, w), lambda bi: (0, 0)),
                  pl.BlockSpec((LANES, LANES), lambda bi: (0, 0))],
        out_specs=pl.BlockSpec((None, s - out_off, w), lambda bi: (bi, 0, 0)),
        out_shape=jax.ShapeDtypeStruct((b, s - out_off, w), BF16),
        scratch_shapes=[pltpu.VMEM((s, w), F32), pltpu.VMEM((s, w), F32), tab,
                        pltpu.VMEM((2, RET_HEADS, CHUNK, CHUNK), F32), tab, tab, tab],
        compiler_params=_cparams(("arbitrary",), VMEM_LIMIT),
        name="retention",
    )(log_decay, rq, rk, rv, gr, norm_g, bd)


def _gelu_tanh(x):
    return 0.5 * x * (1.0 + jnp.tanh(math.sqrt(2.0 / math.pi) * (x + 0.044715 * (x * x * x))))


def _lru_kernel(xu_ref, gu_ref, cw_ref, cb_ref, gw_ref, gb_ref, lam_ref, ng_ref, o_ref,
                u_ref, hf_ref, hb_ref, *, ctx_len, seq_len, out_chunk0):
    c = CHUNK
    n_chunks = seq_len // c
    n_ctx_chunks = ctx_len // c
    row = lax.broadcasted_iota(I32, (c, LRU_W), 0)

    def conv(i, carry):
        r0 = pl.multiple_of(i * c, c)
        prev0 = pl.multiple_of(jnp.maximum(r0 - 8, 0), 8)
        next0 = pl.multiple_of(jnp.minimum(r0 + c, seq_len - 8), 8)
        ext = jnp.concatenate([xu_ref[pl.ds(prev0, 8), :], xu_ref[pl.ds(r0, c), :], xu_ref[pl.ds(next0, 8), :]],
                              axis=0)
        seg_first = jnp.logical_or(i == 0, i == n_ctx_chunks)
        seg_last = jnp.logical_or(i == n_ctx_chunks - 1, i == n_chunks - 1)
        xm2 = pltpu.roll(ext, 2, 0)[8:8 + c]
        xm1 = pltpu.roll(ext, 1, 0)[8:8 + c]
        xp1 = pltpu.roll(ext, c + 16 - 1, 0)[8:8 + c]
        xm2 = jnp.where(jnp.logical_and(seg_first, row < 2), 0.0, xm2)
        xm1 = jnp.where(jnp.logical_and(seg_first, row < 1), 0.0, xm1)
        xp1 = jnp.where(jnp.logical_and(seg_last, row >= c - 1), 0.0, xp1)
        u_ref[pl.ds(r0, c), :] = (cw_ref[0:1, :] * xm2 + cw_ref[1:2, :] * xm1 + cw_ref[2:3, :] * ext[8:8 + c]
                                  + cw_ref[3:4, :] * xp1 + cb_ref[...])
        return carry

    lax.fori_loop(0, n_chunks, conv, 0)

    def softplus(z):
        return jnp.maximum(z, 0.0) + jnp.log1p(jnp.exp(-jnp.abs(z)))

    def scan_dir(d, u):
        ub = u.astype(BF16)
        zr = jnp.dot(ub, gw_ref[d, 0], preferred_element_type=F32) + gb_ref[2 * d:2 * d + 1, :]
        zi = jnp.dot(ub, gw_ref[d, 1], preferred_element_type=F32) + gb_ref[2 * d + 1:2 * d + 2, :]
        log_a = -LRU_C * softplus(-lam_ref[d:d + 1, :]) * _sigmoid(zr)
        a = jnp.exp(log_a)
        b = jnp.sqrt(-jnp.tanh(log_a) * (a * a + 1.0)) * (_sigmoid(zi) * u)
        sh = 1
        while sh < c:
            if d == 0:
                valid = row >= sh
                a_prev = pltpu.roll(a, sh, 0)
                b_prev = pltpu.roll(b, sh, 0)
            else:
                valid = row < c - sh
                a_prev = pltpu.roll(a, c - sh, 0)
                b_prev = pltpu.roll(b, c - sh, 0)
            b = a * jnp.where(valid, b_prev, 0.0) + b
            a = a * jnp.where(valid, a_prev, 1.0)
            sh *= 2
        return a, b

    def step(i, carry):
        h_f, h_b = carry
        r0 = pl.multiple_of(i * c, c)
        a, b = scan_dir(0, u_ref[pl.ds(r0, c), :])
        hh = a * h_f + b
        hf_ref[pl.ds(r0, c), :] = hh
        h_f = hh[c - 1:c, :]
        cb = jnp.where(i < n_ctx_chunks, n_ctx_chunks - 1 - i, n_chunks - 1 - (i - n_ctx_chunks))
        rb = pl.multiple_of(cb * c, c)
        a, b = scan_dir(1, u_ref[pl.ds(rb, c), :])
        hh = a * h_b + b
        hb_ref[pl.ds(rb, c), :] = hh
        h_b = hh[0:1, :]
        return h_f, h_b

    zero = jnp.zeros((1, LRU_W), F32)
    lax.fori_loop(0, n_chunks, step, (zero, zero))

    def finish(i, carry):
        r0 = pl.multiple_of((i + out_chunk0) * c, c)
        ro = pl.multiple_of(i * c, c)
        t = (hf_ref[pl.ds(r0, c), :] + hb_ref[pl.ds(r0, c), :]) * _gelu_tanh(gu_ref[pl.ds(r0, c), :])
        t = t * lax.rsqrt(jnp.mean(t * t, axis=-1, keepdims=True) + EPS) * ng_ref[...]
        o_ref[pl.ds(ro, c), :] = t.astype(BF16)
        return carry

    lax.fori_loop(0, n_chunks - out_chunk0, finish, 0)


def _rglru(xu, gu, conv_w, conv_b, gate_wd, gate_b, lam, norm_g, ctx_len, out_off):
    b, s, w = xu.shape
    kern = functools.partial(_lru_kernel, ctx_len=ctx_len, seq_len=s, out_chunk0=out_off // CHUNK)
    seq = pl.BlockSpec((None, s, w), lambda bi: (bi, 0, 0))
    full = lambda shp: pl.BlockSpec(shp, lambda bi: (0,) * len(shp))
    return pl.pallas_call(
        kern,
        grid=(b,),
        in_specs=[seq, seq, full((4, w)), full((1, w)), full((2, 2, w, w)), full((4, w)), full((2, w)),
                  full((1, w))],
        out_specs=pl.BlockSpec((None, s - out_off, w), lambda bi: (bi, 0, 0)),
        out_shape=jax.ShapeDtypeStruct((b, s - out_off, w), BF16),
        scratch_shapes=[pltpu.VMEM((s, w), F32), pltpu.VMEM((s, w), F32), pltpu.VMEM((s, w), F32)],
        compiler_params=_cparams(("arbitrary",), VMEM_LIMIT),
        name="rglru",
    )(xu, gu, conv_w, conv_b, gate_wd, gate_b, lam, norm_g)


def _outproj_kernel(att_ref, ret_ref, lru_ref, x_ref, mod_ref, ng_ref, w_ref, rw_ref,
                    xmid_ref, h2_ref, pt_ref):
    mix = jnp.concatenate([att_ref[...], ret_ref[...], lru_ref[...]], axis=1)
    y = jnp.dot(mix, w_ref[...], preferred_element_type=F32)
    x = x_ref[...] + mod_ref[2:3, :] * y
    xmid_ref[...] = x
    ms = jnp.mean(x * x, axis=-1, keepdims=True)
    h = (x * lax.rsqrt(ms + EPS) * ng_ref[...]) * (1.0 + mod_ref[4:5, :]) + mod_ref[3:4, :]
    hb = h.astype(BF16)
    h2_ref[...] = hb
    hlo = (h - hb.astype(F32)).astype(BF16)
    whi = rw_ref[0]
    logits = (jnp.dot(hb, whi, preferred_element_type=F32) + jnp.dot(hlo, whi, preferred_element_type=F32)
              + jnp.dot(hb, rw_ref[1], preferred_element_type=F32))
    lane = lax.broadcasted_iota(I32, logits.shape, 1)
    logits = jnp.where(lane < N_EXPERTS, logits, -jnp.inf)
    e = jnp.exp(logits - jnp.max(logits, axis=-1, keepdims=True))
    probs = e / jnp.sum(e, axis=-1, keepdims=True)
    pt_ref[...] = probs.T[0:N_EXPERTS, :]


def _outproj(att, ret, lru, xcat, mod8, norm_g, w_out_b, router2, tile_off, n_ctx_tiles):
    b, so, _ = att.shape
    d = xcat.shape[-1]
    tm = TOK_TILE
    if n_ctx_tiles and tile_off < n_ctx_tiles:
        seg = lambda i: jnp.minimum((i + tile_off) // n_ctx_tiles, 1)
    else:
        seg = lambda i: 1
    tok = lambda w: pl.BlockSpec((None, tm, w), lambda bi, i: (bi, i, 0))
    return pl.pallas_call(
        _outproj_kernel,
        grid=(b, so // tm),
        in_specs=[tok(ATT_W), tok(RET_W), tok(LRU_W),
                  pl.BlockSpec((None, tm, d), lambda bi, i: (bi, i + tile_off, 0)),
                  pl.BlockSpec((None, None, 8, d), lambda bi, i: (bi, seg(i), 0, 0)),
                  pl.BlockSpec((1, d), lambda bi, i: (0, 0)),
                  pl.BlockSpec((d, d), lambda bi, i: (0, 0)),
                  pl.BlockSpec((2, d, LANES), lambda bi, i: (0, 0, 0))],
        out_specs=[tok(d), tok(d), pl.BlockSpec((None, N_EXPERTS, tm), lambda bi, i: (bi, 0, i))],
        out_shape=[jax.ShapeDtypeStruct((b, so, d), F32), jax.ShapeDtypeStruct((b, so, d), BF16),
                   jax.ShapeDtypeStruct((b, N_EXPERTS, so), F32)],
        compiler_params=_cparams(("arbitrary", "arbitrary"), VMEM_LIMIT),
        name="outproj_router",
    )(att, ret, lru, xcat, mod8, norm_g, w_out_b, router2)


def _topk_kernel(p_ref, pos_ref, aff_ref, bnd_ref, *, segs):
    ne = p_ref.shape[0]
    ri = lax.broadcasted_iota(I32, (LANES, LANES), 0)
    ci = lax.broadcasted_iota(I32, (LANES, LANES), 1)
    before = jnp.where(ri < ci, 1.0, 0.0).astype(BF16)
    run_sel = jnp.zeros((ne, 1), F32)

    def kth_floor(vals, k):
        t = jnp.zeros((ne, 1), I32)
        for bit in range(30, -1, -1):
            cand = t | (1 << bit)
            cnt = jnp.sum(jnp.where(vals >= pltpu.bitcast(cand, F32), 1.0, 0.0), axis=1, keepdims=True)
            t = jnp.where(cnt >= k, cand, t)
        return pltpu.bitcast(t, F32)

    for start, ln, k in segs:
        p = p_ref[:, start:start + ln]
        resid = p - kth_floor(p, k)
        t = kth_floor(resid, k)
        gt = resid > t
        eq = resid == t
        need = k - jnp.sum(jnp.where(gt, 1.0, 0.0), axis=1, keepdims=True)
        run_eq = jnp.zeros((ne, 1), F32)
        for j in range(ln // LANES):
            sl = slice(j * LANES, (j + 1) * LANES)
            eq_t = jnp.where(eq[:, sl], 1.0, 0.0)
            rank_eq = jnp.dot(eq_t.astype(BF16), before, preferred_element_type=F32) + run_eq
            sel = jnp.logical_or(gt[:, sl], jnp.logical_and(eq[:, sl], rank_eq < need))
            sel_t = jnp.where(sel, 1.0, 0.0)
            slot = jnp.dot(sel_t.astype(BF16), before, preferred_element_type=F32) + run_sel
            pos_ref[:, start + j * LANES:start + (j + 1) * LANES] = jnp.where(sel, slot, -1.0).astype(I32)
            aff_ref[:, start + j * LANES:start + (j + 1) * LANES] = jnp.where(sel, p[:, sl], 0.0)
            run_eq = run_eq + jnp.sum(eq_t, axis=1, keepdims=True)
            run_sel = run_sel + jnp.sum(sel_t, axis=1, keepdims=True)

    n_slots = sum(k for _, _, k in segs)
    pos = pos_ref[...]
    tok = lax.broadcasted_iota(I32, pos.shape, 1).astype(F32)
    lane = lax.broadcasted_iota(I32, (ne, LANES), 1)
    bounds = jnp.zeros((ne, LANES), I32)
    for i, s0 in enumerate(range(0, n_slots, SLOT_TILE)):
        for off, s in ((0, s0), (MAX_SLOT_TILES, min(s0 + SLOT_TILE, n_slots) - 1)):
            t_at = jnp.sum(jnp.where(pos == s, tok, 0.0), axis=1, keepdims=True).astype(I32)
            bounds = jnp.where(lane == off + i, jnp.right_shift(t_at, TOK_BLOCK.bit_length() - 1), bounds)
    bnd_ref[...] = bounds


def _topk(probs_t, segs):
    b, ne, so = probs_t.shape
    blk = pl.BlockSpec((None, ne, so), lambda bi: (bi, 0, 0))
    return pl.pallas_call(
        functools.partial(_topk_kernel, segs=segs),
        grid=(b,),
        in_specs=[blk],
        out_specs=[blk, blk, pl.BlockSpec((None, ne, LANES), lambda bi: (bi, 0, 0))],
        out_shape=[jax.ShapeDtypeStruct((b, ne, so), I32), jax.ShapeDtypeStruct((b, ne, so), F32),
                   jax.ShapeDtypeStruct((b, ne, LANES), I32)],
        compiler_params=_cparams(("arbitrary",), VMEM_LIMIT),
        name="expert_choice_topk",
    )(probs_t)


def _moe_kernel(bnd_ref, pos_ref, aff_ref, h_ref, wg_ref, wu_ref, wd_ref, o_ref, xg_ref, as_ref, y_ref, *, n_slots):
    e = pl.program_id(1)
    base = (pl.program_id(0) * pl.num_programs(1) + e) * (2 * MAX_SLOT_TILES)
    tiles = [(s0, min(SLOT_TILE, n_slots - s0)) for s0 in range(0, n_slots, SLOT_TILE)]

    def one_hot(s0, rows, c0):
        slot = lax.broadcasted_iota(I32, (rows, TOK_BLOCK), 0) + s0
        return pos_ref[pl.ds(e, 1), pl.ds(c0, TOK_BLOCK)] == slot

    for i, (s0, rows) in enumerate(tiles):
        xg_ref[s0:s0 + rows, :] = jnp.zeros((rows, xg_ref.shape[1]), F32)

        def gather(j, aff_acc, s0=s0, rows=rows):
            c0 = pl.multiple_of(j * TOK_BLOCK, TOK_BLOCK)
            hit = one_hot(s0, rows, c0)
            pm = jnp.where(hit, 1.0, 0.0).astype(BF16)
            xg_ref[s0:s0 + rows, :] += jnp.dot(pm, h_ref[pl.ds(c0, TOK_BLOCK), :], preferred_element_type=F32)
            aff_blk = aff_ref[pl.ds(e, 1), pl.ds(c0, TOK_BLOCK)]
            return aff_acc + jnp.sum(jnp.where(hit, aff_blk, 0.0), axis=1, keepdims=True)

        aff_tile = lax.fori_loop(bnd_ref[base + i], bnd_ref[base + MAX_SLOT_TILES + i] + 1, gather,
                                 jnp.zeros((rows, 1), F32))
        as_ref[s0:s0 + rows, :] = jnp.broadcast_to(aff_tile, (rows, LANES))

    xb = xg_ref[...].astype(BF16)
    gate = jnp.dot(xb, wg_ref[...], preferred_element_type=F32)
    up = jnp.dot(xb, wu_ref[...], preferred_element_type=F32)
    hid = (_silu(gate) * up).astype(BF16)
    y_ref[...] = (jnp.dot(hid, wd_ref[...], preferred_element_type=F32) * as_ref[:, 0:1]).astype(BF16)

    @pl.when(e == 0)
    def _():
        o_ref[...] = jnp.zeros(o_ref.shape, F32)

    tn = (((0,), (0,)), ((), ()))
    for i, (s0, rows) in enumerate(tiles):

        def scatter(j, carry, s0=s0, rows=rows):
            c0 = pl.multiple_of(j * TOK_BLOCK, TOK_BLOCK)
            pm = jnp.where(one_hot(s0, rows, c0), 1.0, 0.0).astype(BF16)
            o_ref[pl.ds(c0, TOK_BLOCK), :] += lax.dot_general(pm, y_ref[s0:s0 + rows, :], tn,
                                                              preferred_element_type=F32)
            return carry

        lax.fori_loop(bnd_ref[base + i], bnd_ref[base + MAX_SLOT_TILES + i] + 1, scatter, 0)


def _moe(bounds, pos, aff, h2, wg, wu, wd, n_slots):
    b, so, d = h2.shape
    ne = wg.shape[0]
    ff = wg.shape[2]
    assert so % TOK_BLOCK == 0 and n_slots <= MAX_SLOT_TILES * SLOT_TILE and n_slots % 16 == 0
    kern = functools.partial(_moe_kernel, n_slots=n_slots)
    sel = pl.BlockSpec((None, ne, so), lambda bi, e, bnd: (bi, 0, 0))
    grid_spec = pltpu.PrefetchScalarGridSpec(
        num_scalar_prefetch=1,
        grid=(b, ne),
        in_specs=[sel, sel,
                  pl.BlockSpec((None, so, d), lambda bi, e, bnd: (bi, 0, 0), pipeline_mode=pl.Buffered(1)),
                  pl.BlockSpec((None, d, ff), lambda bi, e, bnd: (e, 0, 0)),
                  pl.BlockSpec((None, d, ff), lambda bi, e, bnd: (e, 0, 0)),
                  pl.BlockSpec((None, ff, d), lambda bi, e, bnd: (e, 0, 0))],
        out_specs=pl.BlockSpec((None, so, d), lambda bi, e, bnd: (bi, 0, 0), pipeline_mode=pl.Buffered(1)),
        scratch_shapes=[pltpu.VMEM((n_slots, d), F32), pltpu.VMEM((n_slots, LANES), F32),
                        pltpu.VMEM((n_slots, d), BF16)])
    return pl.pallas_call(
        kern,
        grid_spec=grid_spec,
        out_shape=jax.ShapeDtypeStruct((b, so, d), F32),
        compiler_params=_cparams(("arbitrary", "arbitrary"), VMEM_LIMIT),
        name="expert_ffn",
    )(bounds, pos, aff, h2, wg, wu, wd)


def _resid_kernel(x_ref, m_ref, mod_ref, o_ref):
    o_ref[...] = x_ref[...] + mod_ref[5:6, :] * m_ref[...]


def _resid(xmid, moe, mod8, tile_off, n_ctx_tiles):
    b, so, d = xmid.shape
    tm = TOK_TILE
    if n_ctx_tiles and tile_off < n_ctx_tiles:
        seg = lambda i: jnp.minimum((i + tile_off) // n_ctx_tiles, 1)
    else:
        seg = lambda i: 1
    tok = pl.BlockSpec((None, tm, d), lambda bi, i: (bi, i, 0))
    return pl.pallas_call(
        _resid_kernel,
        grid=(b, so // tm),
        in_specs=[tok, tok, pl.BlockSpec((None, None, 8, d), lambda bi, i: (bi, seg(i), 0, 0))],
        out_specs=tok,
        out_shape=jax.ShapeDtypeStruct((b, so, d), F32),
        compiler_params=_cparams(("arbitrary", "arbitrary")),
        name="moe_residual",
    )(xmid, moe, mod8)


def _rope_tables(ctx_len, lat_len):
    rows = lat_len // GRID_W
    row = jnp.repeat(jnp.arange(rows), GRID_W).astype(F32)
    col = jnp.tile(jnp.arange(GRID_W), rows).astype(F32)
    axis_dim = ATT_QK // 2
    inv = 1.0 / (ROPE_BASE ** (jnp.arange(0, axis_dim, 2, dtype=F32) / axis_dim))
    ang_row = row[:, None] * inv
    ang_col = col[:, None] * inv
    z = jnp.zeros_like(ang_row)
    cos64 = jnp.concatenate([jnp.cos(ang_row), jnp.cos(ang_row), jnp.cos(ang_col), jnp.cos(ang_col)], axis=-1)
    up64 = jnp.concatenate([-jnp.sin(ang_row), z, -jnp.sin(ang_col), z], axis=-1)
    dn64 = jnp.concatenate([z, jnp.sin(ang_row), z, jnp.sin(ang_col)], axis=-1)
    lat = jnp.stack([jnp.tile(t, (1, 2)) for t in (cos64, up64, dn64)])
    ctx = jnp.stack([jnp.ones((ctx_len, LANES), F32), jnp.zeros((ctx_len, LANES), F32),
                     jnp.zeros((ctx_len, LANES), F32)])
    return jnp.concatenate([ctx, lat], axis=1)


def kernel(x, c, ctx, c_ctx, mod_w, mod_b, norm1_g, norm2_g, w_in, w_out, att_q_norm_g, att_k_norm_g, att_lambda,
           att_subln_g, ret_log_decay, ret_norm_g, lru_conv_w, lru_conv_b, lru_gate_w, lru_gate_b, lru_lambda,
           lru_norm_g, router_w, exp_w_gate, exp_w_up, exp_w_down):
    b, lat_len, d = x.shape
    ctx_len = ctx.shape[1]
    depth = mod_w.shape[0]
    s = ctx_len + lat_len
    assert ctx_len % TOK_TILE == 0 and lat_len % KEY_BLOCK == 0 and lat_len % GRID_W == 0
    n_ctx_tiles = ctx_len // TOK_TILE

    rows = -(-(b + 1) // 8) * 8
    cc = jnp.zeros((rows, d), F32).at[:b].set(c).at[b].set(c_ctx)
    mod_all = _modulation(cc, mod_w, mod_b).reshape(depth, rows, 6, d)

    rope = _rope_tables(ctx_len, lat_len)
    lane = np.arange(LANES)
    bd = jnp.asarray((lane[:, None] // 64) == (lane[None, :] // 64), BF16)
    xcat = jnp.concatenate([ctx, x], axis=1)

    out = None
    for i in range(depth):
        need_ctx = i < depth - 1
        m6 = mod_all[i]
        mod8 = jnp.stack([jnp.broadcast_to(m6[b], (b, 6, d)), m6[:b]], axis=1)
        mod8 = jnp.pad(mod8, ((0, 0), (0, 0), (0, 2), (0, 0)))
        gqk = jnp.stack([jnp.tile(att_q_norm_g[i], 2), jnp.tile(att_k_norm_g[i], 2)])
        q, kt, v, rq, rk, rv, gr, xu, gu = _inproj(xcat, mod8, norm1_g[i][None], w_in[i].astype(BF16), rope, gqk,
                                                   bd, n_ctx_tiles)
        tile_off = 0 if need_ctx else n_ctx_tiles
        tok_off = tile_off * TOK_TILE
        lam_init = 0.8 - 0.6 * math.exp(-0.3 * i)
        lam_p = jnp.pad(att_lambda[i], ((0, 0), (0, LANES - ATT_QK)))
        att = _attention(lam_p, q, kt, v, att_subln_g[i][None], ctx_len, tile_off, n_ctx_tiles, lam_init)
        ret = _retention(ret_log_decay[i], rq, rk, rv, gr, jnp.tile(ret_norm_g[i], RET_HEADS)[None], bd, ctx_len,
                         tok_off)
        gw = lru_gate_w[i]
        gate_wd = jnp.zeros((2, 2, LRU_W, LRU_W), F32)
        bw = gw.shape[-1]
        for n in range(gw.shape[2]):
            gate_wd = gate_wd.at[:, :, n * bw:(n + 1) * bw, n * bw:(n + 1) * bw].set(gw[:, :, n])
        lru = _rglru(xu, gu, lru_conv_w[i], lru_conv_b[i][None], gate_wd.astype(BF16),
                     lru_gate_b[i].reshape(4, LRU_W), lru_lambda[i], lru_norm_g[i][None], ctx_len, tok_off)
        rw = jnp.pad(router_w[i], ((0, 0), (0, LANES - N_EXPERTS)))
        rw_hi = rw.astype(BF16)
        router2 = jnp.stack([rw_hi, (rw - rw_hi.astype(F32)).astype(BF16)])
        xmid, h2, probs_t = _outproj(att, ret, lru, xcat, mod8, norm2_g[i][None], w_out[i].astype(BF16), router2,
                                     tile_off, n_ctx_tiles)
        segs = []
        if need_ctx:
            segs.append((0, ctx_len, EC_CAPACITY * ctx_len // N_EXPERTS))
        segs.append((ctx_len - tok_off, lat_len, EC_CAPACITY * lat_len // N_EXPERTS))
        pos, aff, bounds = _topk(probs_t, tuple(segs))
        moe = _moe(bounds[:, :, :2 * MAX_SLOT_TILES].reshape(-1), pos, aff, h2, exp_w_gate[i].astype(BF16),
                   exp_w_up[i].astype(BF16), exp_w_down[i].astype(BF16), sum(k for _, _, k in segs))
        out = _resid(xmid, moe, mod8, tile_off, n_ctx_tiles)
        xcat = out
    return out
```

```python
import functools
import math

import jax
import jax.numpy as jnp
import numpy as np
from jax import lax
from jax.experimental import pallas as pl
from jax.experimental.pallas import tpu as pltpu

F32 = jnp.float32
BF16 = jnp.bfloat16
I32 = jnp.int32

ATT_HEADS = 4
ATT_QK = 64
ATT_V = 2 * ATT_QK
ATT_W = ATT_HEADS * ATT_V
RET_HEADS = 4
RET_QK = 64
RET_W = 256
LRU_W = 256
LRU_C = 8.0
N_EXPERTS = 16
EC_CAPACITY = 2
EPS = 1e-6
ROPE_BASE = 10000.0
GRID_W = 64
IN_COLS = 3072

LANES = 128
TOK_TILE = 256
KEY_BLOCK = 512
CHUNK = 128
TOK_BLOCK = 256
SLOT_WIN = 64
SLOT_ALIGN = 16
VMEM_LIMIT = 56 * 1024 * 1024


def _cparams(sem, vmem=None):
    return pltpu.CompilerParams(dimension_semantics=sem, vmem_limit_bytes=vmem)


def _sigmoid(x):
    return 1.0 / (1.0 + jnp.exp(-x))


def _silu(x):
    return x * _sigmoid(x)


def _group_mean_sq(t, bd):
    sq = t * t
    hi = sq.astype(BF16)
    lo = (sq - hi.astype(F32)).astype(BF16)
    ss = jnp.dot(hi, bd, preferred_element_type=F32) + jnp.dot(lo, bd, preferred_element_type=F32)
    return ss * (1.0 / 64.0)


def _mod_kernel(c_ref, w_ref, b_ref, o_ref):
    s = _silu(c_ref[...]).astype(BF16)
    o_ref[...] = jnp.dot(s, w_ref[...].astype(BF16), preferred_element_type=F32) + b_ref[...]


def _modulation(cc, mod_w, mod_b):
    depth, d, d6 = mod_w.shape
    rows = cc.shape[0]
    return pl.pallas_call(
        _mod_kernel,
        grid=(depth, d6 // d),
        in_specs=[pl.BlockSpec((rows, d), lambda l, j: (0, 0)),
                  pl.BlockSpec((None, d, d), lambda l, j: (l, 0, j)),
                  pl.BlockSpec((None, 1, d), lambda l, j: (l, 0, j))],
        out_specs=pl.BlockSpec((None, rows, d), lambda l, j: (l, 0, j)),
        out_shape=jax.ShapeDtypeStruct((depth, rows, d6), F32),
        compiler_params=_cparams(("arbitrary", "arbitrary")),
        name="modulation",
    )(cc, mod_w, mod_b.reshape(depth, 1, d6))


def _token_sources(ctx_src, lat_src, tile_off, n_ctx_tiles, d):
    lat_base = n_ctx_tiles if lat_src is ctx_src else 0
    ctx_spec = pl.BlockSpec((None, TOK_TILE, d), lambda bi, i, *_: (bi, jnp.minimum(i + tile_off, n_ctx_tiles - 1), 0))
    lat_spec = pl.BlockSpec((None, TOK_TILE, d),
                            lambda bi, i, *_: (bi, jnp.maximum(i + tile_off - n_ctx_tiles, 0) + lat_base, 0))
    return ctx_spec, lat_spec


def _inproj_kernel(xc_ref, xl_ref, mod_ref, ng_ref, w_ref, rope_ref, gqk_ref, bd_ref,
                   q_ref, kt_ref, v_ref, rq_ref, rk_ref, rv_ref, gr_ref, xu_ref, gu_ref, *, n_ctx_tiles):
    x = jnp.where(pl.program_id(1) < n_ctx_tiles, xc_ref[...], xl_ref[...])
    ms = jnp.mean(x * x, axis=-1, keepdims=True)
    y = x * lax.rsqrt(ms + EPS) * ng_ref[...]
    h = y * (1.0 + mod_ref[1:2, :]) + mod_ref[0:1, :]
    hb = h.astype(BF16)

    def proj(c0, c1):
        return jnp.dot(hb, w_ref[:, c0:c1], preferred_element_type=F32)

    bd = bd_ref[...]
    cosr = rope_ref[0]
    sin_up = rope_ref[1]
    sin_dn = rope_ref[2]

    def norm_rope(t, ms, g):
        tn = t * lax.rsqrt(ms + EPS) * g
        return tn * cosr + pltpu.roll(tn, LANES - 16, 1) * sin_up + pltpu.roll(tn, 16, 1) * sin_dn

    for hd in range(ATT_HEADS):
        c0 = hd * ATT_V
        tq = proj(c0, c0 + ATT_V)
        tk = proj(ATT_W + c0, ATT_W + c0 + ATT_V)
        ms = _group_mean_sq(jnp.concatenate([tq, tk], axis=1), bd)
        q_ref[:, c0:c0 + ATT_V] = (norm_rope(tq, ms[:, :ATT_V], gqk_ref[0:1, :]) * (ATT_QK ** -0.5)).astype(BF16)
        kt_ref[c0:c0 + ATT_V, :] = norm_rope(tk, ms[:, ATT_V:], gqk_ref[1:2, :]).T.astype(BF16)
    v_ref[...] = proj(2 * ATT_W, 3 * ATT_W).astype(BF16)
    base = 3 * ATT_W
    rq_ref[...] = proj(base, base + RET_W)
    rk_ref[...] = proj(base + RET_W, base + 2 * RET_W) * (RET_QK ** -0.5)
    rv_ref[...] = proj(base + 2 * RET_W, base + 3 * RET_W)
    gr_ref[...] = proj(base + 3 * RET_W, base + 4 * RET_W)
    xu_ref[...] = proj(base + 4 * RET_W, base + 4 * RET_W + LRU_W)
    gu_ref[...] = proj(base + 4 * RET_W + LRU_W, base + 4 * RET_W + 2 * LRU_W)


def _inproj(ctx_src, lat_src, s, mod8, norm_g, w_in_b, rope, gqk, bd, n_ctx_tiles):
    b, _, d = ctx_src.shape
    tm = TOK_TILE
    seg = lambda i: jnp.minimum(i // n_ctx_tiles, 1)
    tok = lambda w: pl.BlockSpec((None, tm, w), lambda bi, i: (bi, i, 0))
    f32o = lambda w: jax.ShapeDtypeStruct((b, s, w), F32)
    ctx_spec, lat_spec = _token_sources(ctx_src, lat_src, 0, n_ctx_tiles, d)
    return pl.pallas_call(
        functools.partial(_inproj_kernel, n_ctx_tiles=n_ctx_tiles),
        grid=(b, s // tm),
        in_specs=[ctx_spec, lat_spec,
                  pl.BlockSpec((None, None, 8, d), lambda bi, i: (bi, seg(i), 0, 0)),
                  pl.BlockSpec((1, d), lambda bi, i: (0, 0)),
                  pl.BlockSpec((d, IN_COLS), lambda bi, i: (0, 0)),
                  pl.BlockSpec((3, tm, LANES), lambda bi, i: (0, i, 0)),
                  pl.BlockSpec((2, LANES), lambda bi, i: (0, 0)),
                  pl.BlockSpec((2 * LANES, 2 * LANES), lambda bi, i: (0, 0))],
        out_specs=[tok(ATT_W),
                   pl.BlockSpec((None, ATT_W, tm), lambda bi, i: (bi, 0, i)),
                   tok(ATT_W), tok(RET_W), tok(RET_W), tok(RET_W), tok(RET_W), tok(LRU_W), tok(LRU_W)],
        out_shape=[jax.ShapeDtypeStruct((b, s, ATT_W), BF16),
                   jax.ShapeDtypeStruct((b, ATT_W, s), BF16),
                   jax.ShapeDtypeStruct((b, s, ATT_W), BF16),
                   f32o(RET_W), f32o(RET_W), f32o(RET_W), f32o(RET_W), f32o(LRU_W), f32o(LRU_W)],
        compiler_params=_cparams(("arbitrary", "arbitrary"), VMEM_LIMIT),
        name="inproj",
    )(ctx_src, lat_src, mod8, norm_g, w_in_b, rope, gqk, bd)


def _attn_kernel(lam_ref, q_ref, kt_ref, v_ref, g_ref, o_ref, *, ctx_len, q_off, n_ctx_tiles, lam_init):
    seq_len = kt_ref.shape[1]
    lp = lam_ref[...]
    lam = (jnp.exp(jnp.sum(lp[0:1, :] * lp[1:2, :], axis=1, keepdims=True))
           - jnp.exp(jnp.sum(lp[2:3, :] * lp[3:4, :], axis=1, keepdims=True)) + lam_init)
    q = q_ref[...]
    lane = lax.broadcasted_iota(I32, q.shape, 1)

    def attend(n_keys):
        outs = []
        for mi in range(2):
            keep = (lane < ATT_QK) if mi == 0 else (lane >= ATT_QK)
            qm = jnp.where(keep, q, jnp.zeros_like(q))
            s = jnp.dot(qm, kt_ref[:, 0:n_keys], preferred_element_type=F32)
            p = jnp.exp(s - jnp.max(s, axis=1, keepdims=True))
            acc = jnp.dot(p.astype(BF16), v_ref[0:n_keys, :], preferred_element_type=F32)
            outs.append(acc / jnp.sum(p, axis=1, keepdims=True))
        o = outs[0] - lam * outs[1]
        o = o * lax.rsqrt(jnp.mean(o * o, axis=-1, keepdims=True) + EPS) * g_ref[...] * (1.0 - lam_init)
        o_ref[...] = o.astype(BF16)

    if q_off >= n_ctx_tiles:
        attend(seq_len)
    else:
        is_ctx = pl.program_id(2) + q_off < n_ctx_tiles
        pl.when(is_ctx)(lambda: attend(ctx_len))
        pl.when(jnp.logical_not(is_ctx))(lambda: attend(seq_len))


def _attention(lam_p, q, kt, v, subln_g, ctx_len, q_off, n_ctx_tiles, lam_init):
    b, s, _ = q.shape
    tq = TOK_TILE
    nq = s // tq - q_off
    kern = functools.partial(_attn_kernel, ctx_len=ctx_len, q_off=q_off, n_ctx_tiles=n_ctx_tiles,
                             lam_init=lam_init)
    return pl.pallas_call(
        kern,
        grid=(b, ATT_HEADS, nq),
        in_specs=[pl.BlockSpec((4, LANES), lambda bi, h, i: (0, 0)),
                  pl.BlockSpec((None, tq, ATT_V), lambda bi, h, i: (bi, i + q_off, h)),
                  pl.BlockSpec((None, ATT_V, s), lambda bi, h, i: (bi, h, 0)),
                  pl.BlockSpec((None, s, ATT_V), lambda bi, h, i: (bi, 0, h)),
                  pl.BlockSpec((1, ATT_V), lambda bi, h, i: (0, 0))],
        out_specs=pl.BlockSpec((None, tq, ATT_V), lambda bi, h, i: (bi, i, h)),
        out_shape=jax.ShapeDtypeStruct((b, nq * tq, ATT_W), BF16),
        compiler_params=_cparams(("arbitrary", "arbitrary", "arbitrary"), VMEM_LIMIT),
        name="diff_attention",
    )(lam_p, q, kt, v, subln_g)


def _ret_kernel(lg_ref, q_ref, k_ref, v_ref, gr_ref, g_ref, bd_ref, o_ref,
                of_ref, ob_ref, st_ref, dm_ref, cr_ref, in_ref, cd_ref, *, n_ctx_chunks, n_chunks, out_chunk0):
    c = CHUNK
    rown = lax.broadcasted_iota(I32, (c, c), 0)
    colm = lax.broadcasted_iota(I32, (c, c), 1)
    low = colm < RET_QK
    rowf = rown.astype(F32)
    for d in range(2):
        if d == 0:
            diff = (rown - colm).astype(F32)
            mask = rown >= colm
            cross_pw = rowf + 1.0
            inner_pw = (c - 1.0) - rowf
        else:
            diff = (colm - rown).astype(F32)
            mask = colm > rown
            cross_pw = c - rowf
            inner_pw = rowf
        for hd in range(RET_HEADS):
            lg = lg_ref[d, hd]
            dm_ref[d, hd] = jnp.where(mask, jnp.exp(lg * jnp.where(mask, diff, 0.0)), 0.0)
        for hp in range(2):
            lg_lane = jnp.where(low, lg_ref[d, 2 * hp], lg_ref[d, 2 * hp + 1])
            cr_ref[d, hp] = jnp.exp(lg_lane * cross_pw)
            in_ref[d, hp] = jnp.exp(lg_lane * inner_pw)
            cd_ref[d, hp] = jnp.exp(lg_lane * float(c))
    st_ref[...] = jnp.zeros(st_ref.shape, F32)
    bdmask = (rown < RET_QK) == (colm < RET_QK)

    def step(i, carry):
        for d in range(2):
            if d == 0:
                ci = i
            else:
                ci = jnp.where(i < n_ctx_chunks, n_ctx_chunks - 1 - i, n_chunks - 1 - (i - n_ctx_chunks))
            r0 = pl.multiple_of(ci * c, c)
            dst = of_ref if d == 0 else ob_ref
            for hp in range(2):
                cols = slice(hp * LANES, (hp + 1) * LANES)
                kf = k_ref[pl.ds(r0, c), cols]
                qb = q_ref[pl.ds(r0, c), cols].astype(BF16)
                kb = kf.astype(BF16)
                vb = v_ref[pl.ds(r0, c), cols].astype(BF16)
                zero = jnp.zeros_like(qb)
                nt = (((1,), (1,)), ((), ()))
                s_a = lax.dot_general(jnp.where(low, qb, zero), kb, nt, preferred_element_type=F32)
                s_b = lax.dot_general(jnp.where(low, zero, qb), kb, nt, preferred_element_type=F32)
                sc = jnp.concatenate([s_a * dm_ref[d, 2 * hp], s_b * dm_ref[d, 2 * hp + 1]], axis=1).astype(BF16)
                vbd = jnp.concatenate([jnp.where(low, vb, zero), jnp.where(low, zero, vb)], axis=0)
                intra = jnp.dot(sc, vbd, preferred_element_type=F32)
                st = st_ref[d, hp]
                inter = jnp.dot(qb, st.astype(BF16), preferred_element_type=F32) * cr_ref[d, hp]
                dst[pl.ds(r0, c), cols] = intra + inter
                kin = (kf * in_ref[d, hp]).astype(BF16)
                kv = lax.dot_general(kin, vb, (((0,), (0,)), ((), ())), preferred_element_type=F32)
                st_ref[d, hp] = st * cd_ref[d, hp] + jnp.where(bdmask, kv, 0.0)
        return carry

    lax.fori_loop(0, n_chunks, step, 0)

    bd = bd_ref[...]

    def finish(i, carry):
        r0 = pl.multiple_of((i + out_chunk0) * c, c)
        ro = pl.multiple_of(i * c, c)
        o = of_ref[pl.ds(r0, c), :] + ob_ref[pl.ds(r0, c), :]
        on = o * lax.rsqrt(_group_mean_sq(o, bd) + EPS) * g_ref[...]
        o_ref[pl.ds(ro, c), :] = (on * _silu(gr_ref[pl.ds(r0, c), :])).astype(BF16)
        return carry

    lax.fori_loop(0, n_chunks - out_chunk0, finish, 0)


def _retention(log_decay, rq, rk, rv, gr, norm_g, bd, ctx_len, out_off):
    b, s, w = rq.shape
    n_chunks = s // CHUNK
    out_chunk0 = out_off // CHUNK
    kern = functools.partial(_ret_kernel, n_ctx_chunks=ctx_len // CHUNK, n_chunks=n_chunks, out_chunk0=out_chunk0)
    seq = pl.BlockSpec((None, s, w), lambda bi: (bi, 0, 0))
    tab = pltpu.VMEM((2, 2, CHUNK, CHUNK), F32)
    return pl.pallas_call(
        kern,
        grid=(b,),
        in_specs=[pl.BlockSpec(memory_space=pltpu.SMEM), seq, seq, seq, seq,
                  pl.BlockSpec((1, w), lambda bi: (0, 0)),
                  pl.BlockSpec((2 * LANES, 2 * LANES), lambda bi: (0, 0))],
        out_specs=pl.BlockSpec((None, s - out_off, w), lambda bi: (bi, 0, 0)),
        out_shape=jax.ShapeDtypeStruct((b, s - out_off, w), BF16),
        scratch_shapes=[pltpu.VMEM((s, w), F32), pltpu.VMEM((s, w), F32), tab,
                        pltpu.VMEM((2, RET_HEADS, CHUNK, CHUNK), F32), tab, tab, tab],
        compiler_params=_cparams(("arbitrary",), VMEM_LIMIT),
        name="retention",
    )(log_decay, rq, rk, rv, gr, norm_g, bd)


def _gelu_tanh(x):
    return 0.5 * x * (1.0 + jnp.tanh(math.sqrt(2.0 / math.pi) * (x + 0.044715 * (x * x * x))))


def _lru_kernel(xu_ref, gu_ref, cw_ref, cb_ref, gw_ref, gb_ref, lam_ref, ng_ref, o_ref,
                u_ref, hf_ref, hb_ref, *, ctx_len, seq_len, out_chunk0):
    c = CHUNK
    n_chunks = seq_len // c
    n_ctx_chunks = ctx_len // c
    row = lax.broadcasted_iota(I32, (c, LRU_W), 0)

    def conv(i, carry):
        r0 = pl.multiple_of(i * c, c)
        prev0 = pl.multiple_of(jnp.maximum(r0 - 8, 0), 8)
        next0 = pl.multiple_of(jnp.minimum(r0 + c, seq_len - 8), 8)
        ext = jnp.concatenate([xu_ref[pl.ds(prev0, 8), :], xu_ref[pl.ds(r0, c), :], xu_ref[pl.ds(next0, 8), :]],
                              axis=0)
        seg_first = jnp.logical_or(i == 0, i == n_ctx_chunks)
        seg_last = jnp.logical_or(i == n_ctx_chunks - 1, i == n_chunks - 1)
        xm2 = pltpu.roll(ext, 2, 0)[8:8 + c]
        xm1 = pltpu.roll(ext, 1, 0)[8:8 + c]
        xp1 = pltpu.roll(ext, c + 16 - 1, 0)[8:8 + c]
        xm2 = jnp.where(jnp.logical_and(seg_first, row < 2), 0.0, xm2)
        xm1 = jnp.where(jnp.logical_and(seg_first, row < 1), 0.0, xm1)
        xp1 = jnp.where(jnp.logical_and(seg_last, row >= c - 1), 0.0, xp1)
        u_ref[pl.ds(r0, c), :] = (cw_ref[0:1, :] * xm2 + cw_ref[1:2, :] * xm1 + cw_ref[2:3, :] * ext[8:8 + c]
                                  + cw_ref[3:4, :] * xp1 + cb_ref[...])
        return carry

    lax.fori_loop(0, n_chunks, conv, 0)

    def softplus(z):
        return jnp.maximum(z, 0.0) + jnp.log1p(jnp.exp(-jnp.abs(z)))

    def scan_dir(d, u):
        ub = u.astype(BF16)
        zr = jnp.dot(ub, gw_ref[d, 0], preferred_element_type=F32) + gb_ref[2 * d:2 * d + 1, :]
        zi = jnp.dot(ub, gw_ref[d, 1], preferred_element_type=F32) + gb_ref[2 * d + 1:2 * d + 2, :]
        log_a = -LRU_C * softplus(-lam_ref[d:d + 1, :]) * _sigmoid(zr)
        a = jnp.exp(log_a)
        b = jnp.sqrt(-jnp.tanh(log_a) * (a * a + 1.0)) * (_sigmoid(zi) * u)
        sh = 1
        while sh < c:
            if d == 0:
                valid = row >= sh
                a_prev = pltpu.roll(a, sh, 0)
                b_prev = pltpu.roll(b, sh, 0)
            else:
                valid = row < c - sh
                a_prev = pltpu.roll(a, c - sh, 0)
                b_prev = pltpu.roll(b, c - sh, 0)
            b = a * jnp.where(valid, b_prev, 0.0) + b
            a = a * jnp.where(valid, a_prev, 1.0)
            sh *= 2
        return a, b

    def step(i, carry):
        h_f, h_b = carry
        r0 = pl.multiple_of(i * c, c)
        a, b = scan_dir(0, u_ref[pl.ds(r0, c), :])
        hh = a * h_f + b
        hf_ref[pl.ds(r0, c), :] = hh
        h_f = hh[c - 1:c, :]
        cb = jnp.where(i < n_ctx_chunks, n_ctx_chunks - 1 - i, n_chunks - 1 - (i - n_ctx_chunks))
        rb = pl.multiple_of(cb * c, c)
        a, b = scan_dir(1, u_ref[pl.ds(rb, c), :])
        hh = a * h_b + b
        hb_ref[pl.ds(rb, c), :] = hh
        h_b = hh[0:1, :]
        return h_f, h_b

    zero = jnp.zeros((1, LRU_W), F32)
    lax.fori_loop(0, n_chunks, step, (zero, zero))

    def finish(i, carry):
        r0 = pl.multiple_of((i + out_chunk0) * c, c)
        ro = pl.multiple_of(i * c, c)
        t = (hf_ref[pl.ds(r0, c), :] + hb_ref[pl.ds(r0, c), :]) * _gelu_tanh(gu_ref[pl.ds(r0, c), :])
        t = t * lax.rsqrt(jnp.mean(t * t, axis=-1, keepdims=True) + EPS) * ng_ref[...]
        o_ref[pl.ds(ro, c), :] = t.astype(BF16)
        return carry

    lax.fori_loop(0, n_chunks - out_chunk0, finish, 0)


def _rglru(xu, gu, conv_w, conv_b, gate_wd, gate_b, lam, norm_g, ctx_len, out_off):
    b, s, w = xu.shape
    kern = functools.partial(_lru_kernel, ctx_len=ctx_len, seq_len=s, out_chunk0=out_off // CHUNK)
    seq = pl.BlockSpec((None, s, w), lambda bi: (bi, 0, 0))
    full = lambda shp: pl.BlockSpec(shp, lambda bi: (0,) * len(shp))
    return pl.pallas_call(
        kern,
        grid=(b,),
        in_specs=[seq, seq, full((4, w)), full((1, w)), full((2, 2, w, w)), full((4, w)), full((2, w)),
                  full((1, w))],
        out_specs=pl.BlockSpec((None, s - out_off, w), lambda bi: (bi, 0, 0)),
        out_shape=jax.ShapeDtypeStruct((b, s - out_off, w), BF16),
        scratch_shapes=[pltpu.VMEM((s, w), F32), pltpu.VMEM((s, w), F32), pltpu.VMEM((s, w), F32)],
        compiler_params=_cparams(("arbitrary",), VMEM_LIMIT),
        name="rglru",
    )(xu, gu, conv_w, conv_b, gate_wd, gate_b, lam, norm_g)


def _outproj_kernel(att_ref, ret_ref, lru_ref, xc_ref, xl_ref, mod_ref, ng_ref, w_ref, rw_ref,
                    xmid_ref, h2_ref, pt_ref, *, tile_off, n_ctx_tiles):
    mix = jnp.concatenate([att_ref[...], ret_ref[...], lru_ref[...]], axis=1)
    y = jnp.dot(mix, w_ref[...], preferred_element_type=F32)
    x_in = jnp.where(pl.program_id(1) + tile_off < n_ctx_tiles, xc_ref[...], xl_ref[...])
    x = x_in + mod_ref[2:3, :] * y
    xmid_ref[...] = x
    ms = jnp.mean(x * x, axis=-1, keepdims=True)
    h = (x * lax.rsqrt(ms + EPS) * ng_ref[...]) * (1.0 + mod_ref[4:5, :]) + mod_ref[3:4, :]
    hb = h.astype(BF16)
    h2_ref[...] = hb
    hlo = (h - hb.astype(F32)).astype(BF16)
    whi = rw_ref[0]
    logits = (jnp.dot(hb, whi, preferred_element_type=F32) + jnp.dot(hlo, whi, preferred_element_type=F32)
              + jnp.dot(hb, rw_ref[1], preferred_element_type=F32))
    lane = lax.broadcasted_iota(I32, logits.shape, 1)
    logits = jnp.where(lane < N_EXPERTS, logits, -jnp.inf)
    e = jnp.exp(logits - jnp.max(logits, axis=-1, keepdims=True))
    probs = e / jnp.sum(e, axis=-1, keepdims=True)
    pt_ref[...] = probs.T[0:N_EXPERTS, :]


def _outproj(att, ret, lru, ctx_src, lat_src, mod8, norm_g, w_out_b, router2, tile_off, n_ctx_tiles):
    b, so, _ = att.shape
    d = ctx_src.shape[-1]
    tm = TOK_TILE
    seg = lambda i: jnp.minimum((i + tile_off) // n_ctx_tiles, 1)
    tok = lambda w: pl.BlockSpec((None, tm, w), lambda bi, i: (bi, i, 0))
    ctx_spec, lat_spec = _token_sources(ctx_src, lat_src, tile_off, n_ctx_tiles, d)
    return pl.pallas_call(
        functools.partial(_outproj_kernel, tile_off=tile_off, n_ctx_tiles=n_ctx_tiles),
        grid=(b, so // tm),
        in_specs=[tok(ATT_W), tok(RET_W), tok(LRU_W), ctx_spec, lat_spec,
                  pl.BlockSpec((None, None, 8, d), lambda bi, i: (bi, seg(i), 0, 0)),
                  pl.BlockSpec((1, d), lambda bi, i: (0, 0)),
                  pl.BlockSpec((d, d), lambda bi, i: (0, 0)),
                  pl.BlockSpec((2, d, LANES), lambda bi, i: (0, 0, 0))],
        out_specs=[tok(d), tok(d), pl.BlockSpec((None, N_EXPERTS, tm), lambda bi, i: (bi, 0, i))],
        out_shape=[jax.ShapeDtypeStruct((b, so, d), F32), jax.ShapeDtypeStruct((b, so, d), BF16),
                   jax.ShapeDtypeStruct((b, N_EXPERTS, so), F32)],
        compiler_params=_cparams(("arbitrary", "arbitrary"), VMEM_LIMIT),
        name="outproj_router",
    )(att, ret, lru, ctx_src, lat_src, mod8, norm_g, w_out_b, router2)


def _topk_kernel(p_ref, pos_ref, aff_ref, win_ref, pages_ref, *, segs):
    ne = p_ref.shape[0]
    block_first_slot = []
    ri = lax.broadcasted_iota(I32, (LANES, LANES), 0)
    ci = lax.broadcasted_iota(I32, (LANES, LANES), 1)
    before = jnp.where(ri < ci, 1.0, 0.0).astype(BF16)
    run_sel = jnp.zeros((ne, 1), F32)

    def kth_floor(vals, k):
        t = jnp.zeros((ne, 1), I32)
        for bit in range(30, -1, -1):
            cand = t | (1 << bit)
            cnt = jnp.sum(jnp.where(vals >= pltpu.bitcast(cand, F32), 1.0, 0.0), axis=1, keepdims=True)
            t = jnp.where(cnt >= k, cand, t)
        return pltpu.bitcast(t, F32)

    for start, ln, k in segs:
        p = p_ref[:, start:start + ln]
        resid = p - kth_floor(p, k)
        t = kth_floor(resid, k)
        gt = resid > t
        eq = resid == t
        need = k - jnp.sum(jnp.where(gt, 1.0, 0.0), axis=1, keepdims=True)
        run_eq = jnp.zeros((ne, 1), F32)
        for j in range(ln // LANES):
            sl = slice(j * LANES, (j + 1) * LANES)
            if (start + j * LANES) % TOK_BLOCK == 0:
                block_first_slot.append(run_sel)
            eq_t = jnp.where(eq[:, sl], 1.0, 0.0)
            rank_eq = jnp.dot(eq_t.astype(BF16), before, preferred_element_type=F32) + run_eq
            sel = jnp.logical_or(gt[:, sl], jnp.logical_and(eq[:, sl], rank_eq < need))
            sel_t = jnp.where(sel, 1.0, 0.0)
            slot = jnp.dot(sel_t.astype(BF16), before, preferred_element_type=F32) + run_sel
            pos_ref[:, start + j * LANES:start + (j + 1) * LANES] = jnp.where(sel, slot, -1.0).astype(I32)
            aff_ref[:, start + j * LANES:start + (j + 1) * LANES] = jnp.where(sel, p[:, sl], 0.0)
            run_eq = run_eq + jnp.sum(eq_t, axis=1, keepdims=True)
            run_sel = run_sel + jnp.sum(sel_t, axis=1, keepdims=True)

    block_first_slot.append(run_sel)
    lane = lax.broadcasted_iota(I32, (ne, LANES), 1)
    win0 = jnp.zeros((ne, LANES), I32)
    pages = jnp.zeros((ne, LANES), I32)
    for j in range(len(block_first_slot) - 1):
        first = block_first_slot[j].astype(I32)
        end = block_first_slot[j + 1].astype(I32)
        w0 = jnp.left_shift(jnp.right_shift(first, SLOT_ALIGN.bit_length() - 1), SLOT_ALIGN.bit_length() - 1)
        need_pages = jnp.right_shift(end - w0 + (SLOT_WIN - 1), SLOT_WIN.bit_length() - 1)
        need_pages = jnp.max(need_pages.astype(F32), axis=0, keepdims=True).astype(I32)
        win0 = jnp.where(lane == j, w0, win0)
        pages = jnp.where(lane == j, need_pages, pages)
    win_ref[...] = win0
    pages_ref[...] = pages


def _topk(probs_t, segs):
    b, ne, so = probs_t.shape
    assert so // TOK_BLOCK <= LANES
    blk = pl.BlockSpec((None, ne, so), lambda bi: (bi, 0, 0))
    tab = pl.BlockSpec((None, ne, LANES), lambda bi: (bi, 0, 0))
    pos, aff, win0, pages = pl.pallas_call(
        functools.partial(_topk_kernel, segs=segs),
        grid=(b,),
        in_specs=[blk],
        out_specs=[blk, blk, tab, tab],
        out_shape=[jax.ShapeDtypeStruct((b, ne, so), I32), jax.ShapeDtypeStruct((b, ne, so), F32),
                   jax.ShapeDtypeStruct((b, ne, LANES), I32), jax.ShapeDtypeStruct((b, ne, LANES), I32)],
        compiler_params=_cparams(("arbitrary",), VMEM_LIMIT),
        name="expert_choice_topk",
    )(probs_t)
    nblk = so // TOK_BLOCK
    win0 = jnp.swapaxes(win0[:, :, :nblk], 1, 2).reshape(-1)
    pages = pages[:, 0, :nblk].reshape(-1)
    return pos, aff, win0, pages


def _slot_windows(win_ref, base, page, n_slots):
    wins = []
    for e in range(N_EXPERTS):
        lo = win_ref[base + e] + page * SLOT_WIN
        wins.append((lo, pl.multiple_of(jnp.minimum(lo, n_slots - SLOT_WIN), SLOT_ALIGN)))
    return wins


def _window_hits(pos, wins):
    row = lax.broadcasted_iota(I32, (SLOT_WIN, TOK_BLOCK), 0)
    hits = []
    for e, (lo, start) in enumerate(wins):
        pe = pos[e:e + 1, :]
        hits.append(jnp.logical_and(pe - start == row, pe >= lo))
    return hits


def _moe_gather_kernel(win_ref, pages_ref, pos_ref, h_ref, xg_ref, *, n_slots):
    j = pl.program_id(1)
    blk = pl.program_id(0) * pl.num_programs(1) + j

    @pl.when(j == 0)
    def _():
        xg_ref[...] = jnp.zeros(xg_ref.shape, BF16)

    pos = pos_ref[...]

    def page(r, carry):
        wins = _slot_windows(win_ref, blk * N_EXPERTS, r, n_slots)
        onehot = jnp.concatenate([jnp.where(hit, 1.0, 0.0).astype(BF16) for hit in _window_hits(pos, wins)], axis=0)
        g = jnp.dot(onehot, h_ref[...], preferred_element_type=F32)
        for e, (_, start) in enumerate(wins):
            xg_ref[e, pl.ds(start, SLOT_WIN), :] += g[e * SLOT_WIN:(e + 1) * SLOT_WIN, :].astype(BF16)
        return carry

    lax.fori_loop(0, pages_ref[blk], page, 0)


def _moe_gather(win0, pages, pos, h2, n_slots):
    b, so, d = h2.shape
    ne = pos.shape[1]
    assert so % TOK_BLOCK == 0 and n_slots >= SLOT_WIN and n_slots % SLOT_ALIGN == 0 and ne == N_EXPERTS
    grid_spec = pltpu.PrefetchScalarGridSpec(
        num_scalar_prefetch=2,
        grid=(b, so // TOK_BLOCK),
        in_specs=[pl.BlockSpec((None, ne, TOK_BLOCK), lambda bi, j, w, p: (bi, 0, j)),
                  pl.BlockSpec((None, TOK_BLOCK, d), lambda bi, j, w, p: (bi, j, 0))],
        out_specs=pl.BlockSpec((None, ne, n_slots, d), lambda bi, j, w, p: (bi, 0, 0, 0)))
    return pl.pallas_call(
        functools.partial(_moe_gather_kernel, n_slots=n_slots),
        grid_spec=grid_spec,
        out_shape=jax.ShapeDtypeStruct((b, ne, n_slots, d), BF16),
        compiler_params=_cparams(("arbitrary", "arbitrary"), VMEM_LIMIT),
        name="moe_gather",
    )(win0, pages, pos, h2)


def _moe_ffn_kernel(x_ref, wg_ref, wu_ref, wd_ref, y_ref, wgb_ref, wub_ref, wdb_ref):
    @pl.when(pl.program_id(1) == 0)
    def _():
        wgb_ref[...] = wg_ref[...].astype(BF16)
        wub_ref[...] = wu_ref[...].astype(BF16)
        wdb_ref[...] = wd_ref[...].astype(BF16)

    xb = x_ref[...]
    gate = jnp.dot(xb, wgb_ref[...], preferred_element_type=F32)
    up = jnp.dot(xb, wub_ref[...], preferred_element_type=F32)
    hid = (_silu(gate) * up).astype(BF16)
    y_ref[...] = jnp.dot(hid, wdb_ref[...], preferred_element_type=F32).astype(BF16)


def _moe_ffn(xg, wg, wu, wd, layer):
    b, ne, n_slots, d = xg.shape
    ff = wg.shape[3]
    rows = pl.BlockSpec((None, None, n_slots, d), lambda e, bi: (bi, e, 0, 0))
    return pl.pallas_call(
        _moe_ffn_kernel,
        grid=(ne, b),
        in_specs=[rows,
                  pl.BlockSpec((None, None, d, ff), lambda e, bi: (layer, e, 0, 0)),
                  pl.BlockSpec((None, None, d, ff), lambda e, bi: (layer, e, 0, 0)),
                  pl.BlockSpec((None, None, ff, d), lambda e, bi: (layer, e, 0, 0))],
        out_specs=rows,
        out_shape=jax.ShapeDtypeStruct((b, ne, n_slots, d), BF16),
        scratch_shapes=[pltpu.VMEM((d, ff), BF16), pltpu.VMEM((d, ff), BF16), pltpu.VMEM((ff, d), BF16)],
        compiler_params=_cparams(("arbitrary", "arbitrary"), VMEM_LIMIT),
        name="expert_ffn",
    )(xg, wg, wu, wd)


def _moe_combine_kernel(win_ref, pages_ref, pos_ref, aff_ref, y_ref, x_ref, mod_ref, o_ref, ycat_ref, acc_ref, *,
                        n_slots):
    blk = pl.program_id(0) * pl.num_programs(1) + pl.program_id(1)
    pos = pos_ref[...]
    aff = aff_ref[...]
    aff_hi = aff.astype(BF16).astype(F32)
    aff_lo = aff - aff_hi
    acc_ref[...] = jnp.zeros(acc_ref.shape, F32)
    tn = (((0,), (0,)), ((), ()))

    def page(r, carry):
        wins = _slot_windows(win_ref, blk * N_EXPERTS, r, n_slots)
        hits = _window_hits(pos, wins)
        for e, (_, start) in enumerate(wins):
            ycat_ref[e * SLOT_WIN:(e + 1) * SLOT_WIN, :] = y_ref[e, pl.ds(start, SLOT_WIN), :]
        w_hi = jnp.concatenate([jnp.where(hit, aff_hi[e:e + 1, :], 0.0).astype(BF16) for e, hit in enumerate(hits)],
                               axis=0)
        w_lo = jnp.concatenate([jnp.where(hit, aff_lo[e:e + 1, :], 0.0).astype(BF16) for e, hit in enumerate(hits)],
                               axis=0)
        ycat = ycat_ref[...]
        acc_ref[...] += (lax.dot_general(w_hi, ycat, tn, preferred_element_type=F32)
                         + lax.dot_general(w_lo, ycat, tn, preferred_element_type=F32))
        return carry

    lax.fori_loop(0, pages_ref[blk], page, 0)
    o_ref[...] = x_ref[...] + mod_ref[5:6, :] * acc_ref[...]


def _moe_combine(win0, pages, pos, aff, y, xmid, mod8, tile_off, n_ctx_tiles):
    b, so, d = xmid.shape
    _, ne, n_slots, _ = y.shape
    assert TOK_BLOCK == TOK_TILE
    if n_ctx_tiles and tile_off < n_ctx_tiles:
        seg = lambda i: jnp.minimum((i + tile_off) // n_ctx_tiles, 1)
    else:
        seg = lambda i: 1
    sel = pl.BlockSpec((None, ne, TOK_BLOCK), lambda bi, j, w, p: (bi, 0, j))
    tok = pl.BlockSpec((None, TOK_BLOCK, d), lambda bi, j, w, p: (bi, j, 0))
    grid_spec = pltpu.PrefetchScalarGridSpec(
        num_scalar_prefetch=2,
        grid=(b, so // TOK_BLOCK),
        in_specs=[sel, sel,
                  pl.BlockSpec((None, ne, n_slots, d), lambda bi, j, w, p: (bi, 0, 0, 0)),
                  tok,
                  pl.BlockSpec((None, None, 8, d), lambda bi, j, w, p: (bi, seg(j), 0, 0))],
        out_specs=tok,
        scratch_shapes=[pltpu.VMEM((ne * SLOT_WIN, d), BF16), pltpu.VMEM((TOK_BLOCK, d), F32)])
    return pl.pallas_call(
        functools.partial(_moe_combine_kernel, n_slots=n_slots),
        grid_spec=grid_spec,
        out_shape=jax.ShapeDtypeStruct((b, so, d), F32),
        compiler_params=_cparams(("arbitrary", "arbitrary"), VMEM_LIMIT),
        name="moe_combine",
    )(win0, pages, pos, aff, y, xmid, mod8)


def _rope_tables(ctx_len, lat_len):
    rows = lat_len // GRID_W
    row = jnp.repeat(jnp.arange(rows), GRID_W).astype(F32)
    col = jnp.tile(jnp.arange(GRID_W), rows).astype(F32)
    axis_dim = ATT_QK // 2
    inv = 1.0 / (ROPE_BASE ** (jnp.arange(0, axis_dim, 2, dtype=F32) / axis_dim))
    ang_row = row[:, None] * inv
    ang_col = col[:, None] * inv
    z = jnp.zeros_like(ang_row)
    cos64 = jnp.concatenate([jnp.cos(ang_row), jnp.cos(ang_row), jnp.cos(ang_col), jnp.cos(ang_col)], axis=-1)
    up64 = jnp.concatenate([-jnp.sin(ang_row), z, -jnp.sin(ang_col), z], axis=-1)
    dn64 = jnp.concatenate([z, jnp.sin(ang_row), z, jnp.sin(ang_col)], axis=-1)
    lat = jnp.stack([jnp.tile(t, (1, 2)) for t in (cos64, up64, dn64)])
    ctx = jnp.stack([jnp.ones((ctx_len, LANES), F32), jnp.zeros((ctx_len, LANES), F32),
                     jnp.zeros((ctx_len, LANES), F32)])
    return jnp.concatenate([ctx, lat], axis=1)


def kernel(x, c, ctx, c_ctx, mod_w, mod_b, norm1_g, norm2_g, w_in, w_out, att_q_norm_g, att_k_norm_g, att_lambda,
           att_subln_g, ret_log_decay, ret_norm_g, lru_conv_w, lru_conv_b, lru_gate_w, lru_gate_b, lru_lambda,
           lru_norm_g, router_w, exp_w_gate, exp_w_up, exp_w_down):
    b, lat_len, d = x.shape
    ctx_len = ctx.shape[1]
    depth = mod_w.shape[0]
    s = ctx_len + lat_len
    assert ctx_len % TOK_TILE == 0 and lat_len % KEY_BLOCK == 0 and lat_len % GRID_W == 0
    n_ctx_tiles = ctx_len // TOK_TILE

    rows = -(-(b + 1) // 8) * 8
    cc = jnp.zeros((rows, d), F32).at[:b].set(c).at[b].set(c_ctx)
    mod_all = _modulation(cc, mod_w, mod_b).reshape(depth, rows, 6, d)

    rope = _rope_tables(ctx_len, lat_len)
    lane = np.arange(2 * LANES)
    bd = jnp.asarray((lane[:, None] // 64) == (lane[None, :] // 64), BF16)
    ctx_src, lat_src = ctx, x

    out = None
    for i in range(depth):
        need_ctx = i < depth - 1
        m6 = mod_all[i]
        mod8 = jnp.stack([jnp.broadcast_to(m6[b], (b, 6, d)), m6[:b]], axis=1)
        mod8 = jnp.pad(mod8, ((0, 0), (0, 0), (0, 2), (0, 0)))
        gqk = jnp.stack([jnp.tile(att_q_norm_g[i], 2), jnp.tile(att_k_norm_g[i], 2)])
        q, kt, v, rq, rk, rv, gr, xu, gu = _inproj(ctx_src, lat_src, s, mod8, norm1_g[i][None],
                                                   w_in[i].astype(BF16), rope, gqk, bd, n_ctx_tiles)
        tile_off = 0 if need_ctx else n_ctx_tiles
        tok_off = tile_off * TOK_TILE
        lam_init = 0.8 - 0.6 * math.exp(-0.3 * i)
        lam_p = jnp.pad(att_lambda[i], ((0, 0), (0, LANES - ATT_QK)))
        att = _attention(lam_p, q, kt, v, att_subln_g[i][None], ctx_len, tile_off, n_ctx_tiles, lam_init)
        ret = _retention(ret_log_decay[i], rq, rk, rv, gr, jnp.tile(ret_norm_g[i], RET_HEADS)[None], bd, ctx_len,
                         tok_off)
        gw = lru_gate_w[i]
        gate_wd = jnp.zeros((2, 2, LRU_W, LRU_W), F32)
        bw = gw.shape[-1]
        for n in range(gw.shape[2]):
            gate_wd = gate_wd.at[:, :, n * bw:(n + 1) * bw, n * bw:(n + 1) * bw].set(gw[:, :, n])
        lru = _rglru(xu, gu, lru_conv_w[i], lru_conv_b[i][None], gate_wd.astype(BF16),
                     lru_gate_b[i].reshape(4, LRU_W), lru_lambda[i], lru_norm_g[i][None], ctx_len, tok_off)
        rw = jnp.pad(router_w[i], ((0, 0), (0, LANES - N_EXPERTS)))
        rw_hi = rw.astype(BF16)
        router2 = jnp.stack([rw_hi, (rw - rw_hi.astype(F32)).astype(BF16)])
        xmid, h2, probs_t = _outproj(att, ret, lru, ctx_src, lat_src, mod8, norm2_g[i][None],
                                     w_out[i].astype(BF16), router2, tile_off, n_ctx_tiles)
        segs = []
        if need_ctx:
            segs.append((0, ctx_len, EC_CAPACITY * ctx_len // N_EXPERTS))
        segs.append((ctx_len - tok_off, lat_len, EC_CAPACITY * lat_len // N_EXPERTS))
        pos, aff, win0, pages = _topk(probs_t, tuple(segs))
        xg = _moe_gather(win0, pages, pos, h2, sum(k for _, _, k in segs))
        y = _moe_ffn(xg, exp_w_gate, exp_w_up, exp_w_down, i)
        out = _moe_combine(win0, pages, pos, aff, y, xmid, mod8, tile_off, n_ctx_tiles)
        ctx_src = lat_src = out
    return out
```

```python
import functools
import math

import jax
import jax.numpy as jnp
import numpy as np
from jax import lax
from jax.experimental import pallas as pl
from jax.experimental.pallas import tpu as pltpu

F32 = jnp.float32
BF16 = jnp.bfloat16
I32 = jnp.int32

ATT_HEADS = 4
ATT_QK = 64
ATT_V = 2 * ATT_QK
ATT_W = ATT_HEADS * ATT_V
RET_HEADS = 4
RET_QK = 64
RET_W = 256
LRU_W = 256
LRU_C = 8.0
N_EXPERTS = 16
EC_CAPACITY = 2
EPS = 1e-6
ROPE_BASE = 10000.0
GRID_W = 64
IN_COLS = 3072
Q_SCALE = ATT_QK ** -0.5 * math.log2(math.e)

LANES = 128
SUBLANES = 8
TOK_TILE = 256
KEY_BLOCK = 512
CHUNK = 128
TOK_BLOCK = 256
SLOT_WIN = 64
SLOT_ALIGN = 16
VMEM_LIMIT = 56 * 1024 * 1024


def _cparams(sem, vmem=None):
    return pltpu.CompilerParams(dimension_semantics=sem, vmem_limit_bytes=vmem)


def _sigmoid(x):
    return 0.5 * (jnp.tanh(0.5 * x) + 1.0)


def _silu(x):
    return x * _sigmoid(x)


def _group_mean_sq(t, bd):
    sq = t * t
    hi = sq.astype(BF16)
    lo = (sq - hi.astype(F32)).astype(BF16)
    ss = jnp.dot(hi, bd, preferred_element_type=F32) + jnp.dot(lo, bd, preferred_element_type=F32)
    return ss * (1.0 / 64.0)


def _mod_kernel(c_ref, w_ref, b_ref, o_ref):
    s = _silu(c_ref[...]).astype(BF16)
    o_ref[...] = jnp.dot(s, w_ref[...].astype(BF16), preferred_element_type=F32) + b_ref[...]


def _modulation(cc, mod_w, mod_b):
    depth, d, d6 = mod_w.shape
    rows = cc.shape[0]
    return pl.pallas_call(
        _mod_kernel,
        grid=(depth, d6 // d),
        in_specs=[pl.BlockSpec((rows, d), lambda l, j: (0, 0)),
                  pl.BlockSpec((None, d, d), lambda l, j: (l, 0, j)),
                  pl.BlockSpec((None, 1, d), lambda l, j: (l, 0, j))],
        out_specs=pl.BlockSpec((None, rows, d), lambda l, j: (l, 0, j)),
        out_shape=jax.ShapeDtypeStruct((depth, rows, d6), F32),
        compiler_params=_cparams(("arbitrary", "arbitrary")),
        name="modulation",
    )(cc, mod_w, mod_b.reshape(depth, 1, d6))


def _samples_per_step(b):
    return 2 if b % 2 == 0 else 1


def _token_sources(ctx_src, lat_src, tile_off, n_ctx_tiles, d, sb):
    lat_base = n_ctx_tiles if lat_src is ctx_src else 0
    ctx_spec = pl.BlockSpec((sb, TOK_TILE, d), lambda bi, i, *_: (bi, jnp.minimum(i + tile_off, n_ctx_tiles - 1), 0))
    lat_spec = pl.BlockSpec((sb, TOK_TILE, d),
                            lambda bi, i, *_: (bi, jnp.maximum(i + tile_off - n_ctx_tiles, 0) + lat_base, 0))
    return ctx_spec, lat_spec


def _put_rows(ref, val, cols=slice(None)):
    rows = ref.shape[1]
    for si in range(ref.shape[0]):
        ref[si, :, cols] = val[si * rows:(si + 1) * rows].astype(ref.dtype)


def _inproj_kernel(xc_ref, xl_ref, mod_ref, ng_ref, w_ref, rope_ref, gqk_ref, bd_ref,
                   q_ref, kt_ref, v_ref, rq_ref, rk_ref, rv_ref, gr_ref, xu_ref, gu_ref, *, n_ctx_tiles):
    sb, tm, _ = xc_ref.shape
    is_ctx = pl.program_id(1) < n_ctx_tiles
    hs = []
    for si in range(sb):
        x = jnp.where(is_ctx, xc_ref[si], xl_ref[si])
        ms = jnp.mean(x * x, axis=-1, keepdims=True)
        y = x * lax.rsqrt(ms + EPS) * ng_ref[...]
        hs.append((y * (1.0 + mod_ref[si, 1:2, :]) + mod_ref[si, 0:1, :]).astype(BF16))
    hb = jnp.concatenate(hs, axis=0)

    def proj(c0, c1):
        return jnp.dot(hb, w_ref[:, c0:c1], preferred_element_type=F32)

    bd = bd_ref[...]
    cosr = rope_ref[0]
    sin_up = rope_ref[1]
    sin_dn = rope_ref[2]

    def norm_rope(t, ms, g):
        tn = t * lax.rsqrt(ms + EPS) * g
        return tn * cosr + pltpu.roll(tn, LANES - 16, 1) * sin_up + pltpu.roll(tn, 16, 1) * sin_dn

    for hd in range(ATT_HEADS):
        c0 = hd * ATT_V
        tq = proj(c0, c0 + ATT_V)
        tk = proj(ATT_W + c0, ATT_W + c0 + ATT_V)
        ms = _group_mean_sq(jnp.concatenate([tq, tk], axis=1), bd)
        for si in range(sb):
            rows = slice(si * tm, (si + 1) * tm)
            q_ref[si, :, c0:c0 + ATT_V] = (norm_rope(tq[rows], ms[rows, :ATT_V], gqk_ref[0:1, :])
                                           * Q_SCALE).astype(BF16)
            kt_ref[si, c0:c0 + ATT_V, :] = norm_rope(tk[rows], ms[rows, ATT_V:], gqk_ref[1:2, :]).T.astype(BF16)
    _put_rows(v_ref, proj(2 * ATT_W, 3 * ATT_W))
    base = 3 * ATT_W
    _put_rows(rq_ref, proj(base, base + RET_W))
    _put_rows(rk_ref, proj(base + RET_W, base + 2 * RET_W) * (RET_QK ** -0.5))
    _put_rows(rv_ref, proj(base + 2 * RET_W, base + 3 * RET_W))
    _put_rows(gr_ref, proj(base + 3 * RET_W, base + 4 * RET_W))
    _put_rows(xu_ref, proj(base + 4 * RET_W, base + 4 * RET_W + LRU_W))
    _put_rows(gu_ref, proj(base + 4 * RET_W + LRU_W, base + 4 * RET_W + 2 * LRU_W))


def _inproj(ctx_src, lat_src, s, mod8, norm_g, w_in_b, rope, gqk, bd, n_ctx_tiles):
    b, _, d = ctx_src.shape
    tm = TOK_TILE
    sb = _samples_per_step(b)
    seg = lambda i: jnp.minimum(i // n_ctx_tiles, 1)
    tok = lambda w: pl.BlockSpec((sb, tm, w), lambda bi, i: (bi, i, 0))
    f32o = lambda w: jax.ShapeDtypeStruct((b, s, w), F32)
    ctx_spec, lat_spec = _token_sources(ctx_src, lat_src, 0, n_ctx_tiles, d, sb)
    return pl.pallas_call(
        functools.partial(_inproj_kernel, n_ctx_tiles=n_ctx_tiles),
        grid=(b // sb, s // tm),
        in_specs=[ctx_spec, lat_spec,
                  pl.BlockSpec((sb, None, 8, d), lambda bi, i: (bi, seg(i), 0, 0)),
                  pl.BlockSpec((1, d), lambda bi, i: (0, 0)),
                  pl.BlockSpec((d, IN_COLS), lambda bi, i: (0, 0)),
                  pl.BlockSpec((3, tm, LANES), lambda bi, i: (0, i, 0)),
                  pl.BlockSpec((2, LANES), lambda bi, i: (0, 0)),
                  pl.BlockSpec((2 * LANES, 2 * LANES), lambda bi, i: (0, 0))],
        out_specs=[tok(ATT_W),
                   pl.BlockSpec((sb, ATT_W, tm), lambda bi, i: (bi, 0, i)),
                   tok(ATT_W), tok(RET_W), tok(RET_W), tok(RET_W), tok(RET_W), tok(LRU_W), tok(LRU_W)],
        out_shape=[jax.ShapeDtypeStruct((b, s, ATT_W), BF16),
                   jax.ShapeDtypeStruct((b, ATT_W, s), BF16),
                   jax.ShapeDtypeStruct((b, s, ATT_W), BF16),
                   f32o(RET_W), f32o(RET_W), f32o(RET_W), f32o(RET_W), f32o(LRU_W), f32o(LRU_W)],
        compiler_params=_cparams(("arbitrary", "arbitrary"), VMEM_LIMIT),
        name="inproj",
    )(ctx_src, lat_src, mod8, norm_g, w_in_b, rope, gqk, bd)


def _attn_kernel(lam_ref, q_ref, kt_ref, v_ref, g_ref, o_ref, *, ctx_len, q_off, n_ctx_tiles, lam_init):
    seq_len = kt_ref.shape[1]
    lp = lam_ref[...]
    lam = (jnp.exp(jnp.sum(lp[0:1, :] * lp[1:2, :], axis=1, keepdims=True))
           - jnp.exp(jnp.sum(lp[2:3, :] * lp[3:4, :], axis=1, keepdims=True)) + lam_init)
    q = q_ref[...]
    lane = lax.broadcasted_iota(I32, q.shape, 1)

    def attend(n_keys):
        outs = []
        for mi in range(2):
            keep = (lane < ATT_QK) if mi == 0 else (lane >= ATT_QK)
            qm = jnp.where(keep, q, jnp.zeros_like(q))
            s = jnp.dot(qm, kt_ref[:, 0:n_keys], preferred_element_type=F32)
            p = jnp.exp2(s - jnp.max(s, axis=1, keepdims=True))
            outs.append((p, jnp.sum(p, axis=1, keepdims=True)))
        a = outs[0][0] * (1.0 / outs[0][1]) - outs[1][0] * (lam / outs[1][1])
        o = jnp.dot(a.astype(BF16), v_ref[0:n_keys, :], preferred_element_type=F32)
        o = o * lax.rsqrt(jnp.mean(o * o, axis=-1, keepdims=True) + EPS) * g_ref[...] * (1.0 - lam_init)
        o_ref[...] = o.astype(BF16)

    if q_off >= n_ctx_tiles:
        attend(seq_len)
    else:
        is_ctx = pl.program_id(2) + q_off < n_ctx_tiles
        pl.when(is_ctx)(lambda: attend(ctx_len))
        pl.when(jnp.logical_not(is_ctx))(lambda: attend(seq_len))


def _attention(lam_p, q, kt, v, subln_g, ctx_len, q_off, n_ctx_tiles, lam_init):
    b, s, _ = q.shape
    tq = TOK_TILE
    nq = s // tq - q_off
    kern = functools.partial(_attn_kernel, ctx_len=ctx_len, q_off=q_off, n_ctx_tiles=n_ctx_tiles,
                             lam_init=lam_init)
    return pl.pallas_call(
        kern,
        grid=(b, ATT_HEADS, nq),
        in_specs=[pl.BlockSpec((4, LANES), lambda bi, h, i: (0, 0)),
                  pl.BlockSpec((None, tq, ATT_V), lambda bi, h, i: (bi, i + q_off, h)),
                  pl.BlockSpec((None, ATT_V, s), lambda bi, h, i: (bi, h, 0)),
                  pl.BlockSpec((None, s, ATT_V), lambda bi, h, i: (bi, 0, h)),
                  pl.BlockSpec((1, ATT_V), lambda bi, h, i: (0, 0))],
        out_specs=pl.BlockSpec((None, tq, ATT_V), lambda bi, h, i: (bi, i, h)),
        out_shape=jax.ShapeDtypeStruct((b, nq * tq, ATT_W), BF16),
        compiler_params=_cparams(("arbitrary", "arbitrary", "arbitrary"), VMEM_LIMIT),
        name="diff_attention",
    )(lam_p, q, kt, v, subln_g)


def _ret_kernel(lg_ref, q_ref, k_ref, v_ref, gr_ref, g_ref, bd_ref, o_ref,
                of_ref, ob_ref, st_ref, dm_ref, cr_ref, in_ref, cd_ref, *, n_ctx_chunks, n_chunks, out_chunk0):
    c = CHUNK
    rown = lax.broadcasted_iota(I32, (c, c), 0)
    colm = lax.broadcasted_iota(I32, (c, c), 1)
    low = colm < RET_QK
    rowf = rown.astype(F32)
    for d in range(2):
        if d == 0:
            diff = (rown - colm).astype(F32)
            mask = rown >= colm
            cross_pw = rowf + 1.0
            inner_pw = (c - 1.0) - rowf
        else:
            diff = (colm - rown).astype(F32)
            mask = colm > rown
            cross_pw = c - rowf
            inner_pw = rowf
        for hp in range(2):
            dm_ref[d, hp] = jnp.concatenate(
                [jnp.where(mask, jnp.exp(lg_ref[d, 2 * hp + j] * jnp.where(mask, diff, 0.0)), 0.0) for j in range(2)],
                axis=1)
            lg_lane = jnp.where(low, lg_ref[d, 2 * hp], lg_ref[d, 2 * hp + 1])
            cr_ref[d, hp] = jnp.exp(lg_lane * cross_pw)
            in_ref[d, hp] = jnp.exp(lg_lane * inner_pw)
            cd_ref[d, hp] = jnp.exp(lg_lane * float(c))
    st_ref[...] = jnp.zeros(st_ref.shape, F32)
    bdmask = (rown < RET_QK) == (colm < RET_QK)

    def step(i, carry):
        for d in range(2):
            if d == 0:
                ci = i
            else:
                ci = jnp.where(i < n_ctx_chunks, n_ctx_chunks - 1 - i, n_chunks - 1 - (i - n_ctx_chunks))
            r0 = pl.multiple_of(ci * c, c)
            dst = of_ref if d == 0 else ob_ref
            for hp in range(2):
                cols = slice(hp * LANES, (hp + 1) * LANES)
                kf = k_ref[pl.ds(r0, c), cols]
                qf = q_ref[pl.ds(r0, c), cols]
                kb = kf.astype(BF16)
                vb = v_ref[pl.ds(r0, c), cols].astype(BF16)
                zero = jnp.zeros_like(kb)
                nt = (((1,), (1,)), ((), ()))
                kcat = jnp.concatenate([jnp.where(low, kb, zero), jnp.where(low, zero, kb)], axis=0)
                sc = (lax.dot_general(qf.astype(BF16), kcat, nt, preferred_element_type=F32)
                      * dm_ref[d, hp]).astype(BF16)
                vbd = jnp.concatenate([jnp.where(low, vb, zero), jnp.where(low, zero, vb)], axis=0)
                st = st_ref[d, hp]
                intra = jnp.dot(sc, vbd, preferred_element_type=F32)
                inter = jnp.dot(qf.astype(BF16), st.astype(BF16), preferred_element_type=F32) * cr_ref[d, hp]
                dst[pl.ds(r0, c), cols] = intra + inter
                kin = (kf * in_ref[d, hp]).astype(BF16)
                kv = lax.dot_general(kin, vb, (((0,), (0,)), ((), ())), preferred_element_type=F32)
                st_ref[d, hp] = st * cd_ref[d, hp] + jnp.where(bdmask, kv, 0.0)
        return carry

    lax.fori_loop(0, n_chunks, step, 0, unroll=2)

    bd = bd_ref[...]

    def finish(i, carry):
        r0 = pl.multiple_of((i + out_chunk0) * c, c)
        ro = pl.multiple_of(i * c, c)
        o = of_ref[pl.ds(r0, c), :] + ob_ref[pl.ds(r0, c), :]
        on = o * lax.rsqrt(_group_mean_sq(o, bd) + EPS) * g_ref[...]
        o_ref[pl.ds(ro, c), :] = (on * _silu(gr_ref[pl.ds(r0, c), :])).astype(BF16)
        return carry

    lax.fori_loop(0, n_chunks - out_chunk0, finish, 0)


def _retention(log_decay, rq, rk, rv, gr, norm_g, bd, ctx_len, out_off):
    b, s, w = rq.shape
    n_chunks = s // CHUNK
    out_chunk0 = out_off // CHUNK
    kern = functools.partial(_ret_kernel, n_ctx_chunks=ctx_len // CHUNK, n_chunks=n_chunks, out_chunk0=out_chunk0)
    seq = pl.BlockSpec((None, s, w), lambda bi: (bi, 0, 0))
    tab = pltpu.VMEM((2, 2, CHUNK, CHUNK), F32)
    return pl.pallas_call(
        kern,
        grid=(b,),
        in_specs=[pl.BlockSpec(memory_space=pltpu.SMEM), seq, seq, seq, seq,
                  pl.BlockSpec((1, w), lambda bi: (0, 0)),
                  pl.BlockSpec((2 * LANES, 2 * LANES), lambda bi: (0, 0))],
        out_specs=pl.BlockSpec((None, s - out_off, w), lambda bi: (bi, 0, 0)),
        out_shape=jax.ShapeDtypeStruct((b, s - out_off, w), BF16),
        scratch_shapes=[pltpu.VMEM((s, w), F32), pltpu.VMEM((s, w), F32), tab,
                        pltpu.VMEM((2, 2, CHUNK, 2 * CHUNK), F32), tab, tab, tab],
        compiler_params=_cparams(("arbitrary",), VMEM_LIMIT),
        name="retention",
    )(log_decay, rq, rk, rv, gr, norm_g, bd)


def _gelu_tanh(x):
    return 0.5 * x * (1.0 + jnp.tanh(math.sqrt(2.0 / math.pi) * (x + 0.044715 * (x * x * x))))


def _lru_kernel(xu_ref, gu_ref, cw_ref, cb_ref, gw_ref, gb_ref, lam_ref, ng_ref, o_ref,
                u_ref, hf_ref, hb_ref, *, ctx_len, seq_len, out_chunk0):
    c = CHUNK
    n_chunks = seq_len // c
    n_ctx_chunks = ctx_len // c
    row = lax.broadcasted_iota(I32, (c, LRU_W), 0)

    def conv(i, carry):
        r0 = pl.multiple_of(i * c, c)
        prev0 = pl.multiple_of(jnp.maximum(r0 - 8, 0), 8)
        next0 = pl.multiple_of(jnp.minimum(r0 + c, seq_len - 8), 8)
        ext = jnp.concatenate([xu_ref[pl.ds(prev0, 8), :], xu_ref[pl.ds(r0, c), :], xu_ref[pl.ds(next0, 8), :]],
                              axis=0)
        seg_first = jnp.logical_or(i == 0, i == n_ctx_chunks)
        seg_last = jnp.logical_or(i == n_ctx_chunks - 1, i == n_chunks - 1)
        xm2 = pltpu.roll(ext, 2, 0)[8:8 + c]
        xm1 = pltpu.roll(ext, 1, 0)[8:8 + c]
        xp1 = pltpu.roll(ext, c + 16 - 1, 0)[8:8 + c]
        xm2 = jnp.where(jnp.logical_and(seg_first, row < 2), 0.0, xm2)
        xm1 = jnp.where(jnp.logical_and(seg_first, row < 1), 0.0, xm1)
        xp1 = jnp.where(jnp.logical_and(seg_last, row >= c - 1), 0.0, xp1)
        u_ref[pl.ds(r0, c), :] = (cw_ref[0:1, :] * xm2 + cw_ref[1:2, :] * xm1 + cw_ref[2:3, :] * ext[8:8 + c]
                                  + cw_ref[3:4, :] * xp1 + cb_ref[...])
        return carry

    lax.fori_loop(0, n_chunks, conv, 0)

    def softplus(z):
        return jnp.maximum(z, 0.0) + jnp.log1p(jnp.exp(-jnp.abs(z)))

    def scan_dir(d, u):
        ub = u.astype(BF16)
        zr = jnp.dot(ub, gw_ref[d, 0], preferred_element_type=F32) + gb_ref[2 * d:2 * d + 1, :]
        zi = jnp.dot(ub, gw_ref[d, 1], preferred_element_type=F32) + gb_ref[2 * d + 1:2 * d + 2, :]
        log_a = -LRU_C * softplus(-lam_ref[d:d + 1, :]) * _sigmoid(zr)
        a = jnp.exp(log_a)
        b = jnp.sqrt(-jnp.tanh(log_a) * (a * a + 1.0)) * (_sigmoid(zi) * u)
        ng = c // SUBLANES
        a = a.reshape(ng, SUBLANES, LRU_W)
        b = b.reshape(ng, SUBLANES, LRU_W)
        sub = lax.broadcasted_iota(I32, (ng, SUBLANES, LRU_W), 1)
        sh = 1
        while sh < SUBLANES:
            if d == 0:
                valid = sub >= sh
                a_prev = pltpu.roll(a, sh, 1)
                b_prev = pltpu.roll(b, sh, 1)
            else:
                valid = sub < SUBLANES - sh
                a_prev = pltpu.roll(a, SUBLANES - sh, 1)
                b_prev = pltpu.roll(b, SUBLANES - sh, 1)
            b = a * jnp.where(valid, b_prev, 0.0) + b
            a = a * jnp.where(valid, a_prev, 1.0)
            sh *= 2
        return a, b

    def chain(d, a, b, h):
        ng = c // SUBLANES
        out = [None] * ng
        for g in (range(ng) if d == 0 else range(ng - 1, -1, -1)):
            hg = a[g] * h + b[g]
            out[g] = hg
            h = hg[SUBLANES - 1:SUBLANES, :] if d == 0 else hg[0:1, :]
        return jnp.concatenate(out, axis=0), h

    def step(i, carry):
        h_f, h_b = carry
        r0 = pl.multiple_of(i * c, c)
        a, b = scan_dir(0, u_ref[pl.ds(r0, c), :])
        hh, h_f = chain(0, a, b, h_f)
        hf_ref[pl.ds(r0, c), :] = hh
        cb = jnp.where(i < n_ctx_chunks, n_ctx_chunks - 1 - i, n_chunks - 1 - (i - n_ctx_chunks))
        rb = pl.multiple_of(cb * c, c)
        a, b = scan_dir(1, u_ref[pl.ds(rb, c), :])
        hh, h_b = chain(1, a, b, h_b)
        hb_ref[pl.ds(rb, c), :] = hh
        return h_f, h_b

    zero = jnp.zeros((1, LRU_W), F32)
    lax.fori_loop(0, n_chunks, step, (zero, zero))

    def finish(i, carry):
        r0 = pl.multiple_of((i + out_chunk0) * c, c)
        ro = pl.multiple_of(i * c, c)
        t = (hf_ref[pl.ds(r0, c), :] + hb_ref[pl.ds(r0, c), :]) * _gelu_tanh(gu_ref[pl.ds(r0, c), :])
        t = t * lax.rsqrt(jnp.mean(t * t, axis=-1, keepdims=True) + EPS) * ng_ref[...]
        o_ref[pl.ds(ro, c), :] = t.astype(BF16)
        return carry

    lax.fori_loop(0, n_chunks - out_chunk0, finish, 0)


def _rglru(xu, gu, conv_w, conv_b, gate_wd, gate_b, lam, norm_g, ctx_len, out_off):
    b, s, w = xu.shape
    kern = functools.partial(_lru_kernel, ctx_len=ctx_len, seq_len=s, out_chunk0=out_off // CHUNK)
    seq = pl.BlockSpec((None, s, w), lambda bi: (bi, 0, 0))
    full = lambda shp: pl.BlockSpec(shp, lambda bi: (0,) * len(shp))
    return pl.pallas_call(
        kern,
        grid=(b,),
        in_specs=[seq, seq, full((4, w)), full((1, w)), full((2, 2, w, w)), full((4, w)), full((2, w)),
                  full((1, w))],
        out_specs=pl.BlockSpec((None, s - out_off, w), lambda bi: (bi, 0, 0)),
        out_shape=jax.ShapeDtypeStruct((b, s - out_off, w), BF16),
        scratch_shapes=[pltpu.VMEM((s, w), F32), pltpu.VMEM((s, w), F32), pltpu.VMEM((s, w), F32)],
        compiler_params=_cparams(("arbitrary",), VMEM_LIMIT),
        name="rglru",
    )(xu, gu, conv_w, conv_b, gate_wd, gate_b, lam, norm_g)


def _outproj_kernel(att_ref, ret_ref, lru_ref, xc_ref, xl_ref, mod_ref, ng_ref, w_ref, rw_ref,
                    xmid_ref, h2_ref, pt_ref, *, tile_off, n_ctx_tiles):
    sb, tm, _ = att_ref.shape
    mix = jnp.concatenate([jnp.concatenate([att_ref[si], ret_ref[si], lru_ref[si]], axis=1) for si in range(sb)],
                          axis=0)
    y = jnp.dot(mix, w_ref[...], preferred_element_type=F32)
    is_ctx = pl.program_id(1) + tile_off < n_ctx_tiles
    hs = []
    for si in range(sb):
        x = jnp.where(is_ctx, xc_ref[si], xl_ref[si]) + mod_ref[si, 2:3, :] * y[si * tm:(si + 1) * tm]
        xmid_ref[si] = x
        ms = jnp.mean(x * x, axis=-1, keepdims=True)
        hs.append((x * lax.rsqrt(ms + EPS) * ng_ref[...]) * (1.0 + mod_ref[si, 4:5, :]) + mod_ref[si, 3:4, :])
    h = jnp.concatenate(hs, axis=0)
    hb = h.astype(BF16)
    _put_rows(h2_ref, hb)
    hlo = (h - hb.astype(F32)).astype(BF16)
    whi = rw_ref[0]
    logits = (jnp.dot(hb, whi, preferred_element_type=F32) + jnp.dot(hlo, whi, preferred_element_type=F32)
              + jnp.dot(hb, rw_ref[1], preferred_element_type=F32))
    lane = lax.broadcasted_iota(I32, logits.shape, 1)
    logits = jnp.where(lane < N_EXPERTS, logits, -jnp.inf)
    e = jnp.exp(logits - jnp.max(logits, axis=-1, keepdims=True))
    probs = e / jnp.sum(e, axis=-1, keepdims=True)
    for si in range(sb):
        pt_ref[si] = probs[si * tm:(si + 1) * tm].T[0:N_EXPERTS, :]


def _outproj(att, ret, lru, ctx_src, lat_src, mod8, norm_g, w_out_b, router2, tile_off, n_ctx_tiles):
    b, so, _ = att.shape
    d = ctx_src.shape[-1]
    tm = TOK_TILE
    sb = 1
    seg = lambda i: jnp.minimum((i + tile_off) // n_ctx_tiles, 1)
    tok = lambda w: pl.BlockSpec((sb, tm, w), lambda bi, i: (bi, i, 0))
    ctx_spec, lat_spec = _token_sources(ctx_src, lat_src, tile_off, n_ctx_tiles, d, sb)
    return pl.pallas_call(
        functools.partial(_outproj_kernel, tile_off=tile_off, n_ctx_tiles=n_ctx_tiles),
        grid=(b // sb, so // tm),
        in_specs=[tok(ATT_W), tok(RET_W), tok(LRU_W), ctx_spec, lat_spec,
                  pl.BlockSpec((sb, None, 8, d), lambda bi, i: (bi, seg(i), 0, 0)),
                  pl.BlockSpec((1, d), lambda bi, i: (0, 0)),
                  pl.BlockSpec((d, d), lambda bi, i: (0, 0)),
                  pl.BlockSpec((2, d, LANES), lambda bi, i: (0, 0, 0))],
        out_specs=[tok(d), tok(d), pl.BlockSpec((sb, N_EXPERTS, tm), lambda bi, i: (bi, 0, i))],
        out_shape=[jax.ShapeDtypeStruct((b, so, d), F32), jax.ShapeDtypeStruct((b, so, d), BF16),
                   jax.ShapeDtypeStruct((b, N_EXPERTS, so), F32)],
        compiler_params=_cparams(("arbitrary", "arbitrary"), VMEM_LIMIT),
        name="outproj_router",
    )(att, ret, lru, ctx_src, lat_src, mod8, norm_g, w_out_b, router2)


def _topk_kernel(p_ref, pos_ref, aff_ref, win_ref, pages_ref, *, segs):
    ne = p_ref.shape[0]
    block_first_slot = []
    ri = lax.broadcasted_iota(I32, (LANES, LANES), 0)
    ci = lax.broadcasted_iota(I32, (LANES, LANES), 1)
    before = jnp.where(ri < ci, 1.0, 0.0).astype(BF16)
    run_sel = jnp.zeros((ne, 1), F32)

    def kth_floor(vals, k):
        t = jnp.zeros((ne, 1), I32)
        for bit in range(30, -1, -1):
            cand = t | (1 << bit)
            cnt = jnp.sum(jnp.where(vals >= pltpu.bitcast(cand, F32), 1.0, 0.0), axis=1, keepdims=True)
            t = jnp.where(cnt >= k, cand, t)
        return pltpu.bitcast(t, F32)

    for start, ln, k in segs:
        p = p_ref[:, start:start + ln]
        resid = p - kth_floor(p, k)
        t = kth_floor(resid, k)
        gt = resid > t
        eq = resid == t
        need = k - jnp.sum(jnp.where(gt, 1.0, 0.0), axis=1, keepdims=True)
        run_eq = jnp.zeros((ne, 1), F32)
        for j in range(ln // LANES):
            sl = slice(j * LANES, (j + 1) * LANES)
            if (start + j * LANES) % TOK_BLOCK == 0:
                block_first_slot.append(run_sel)
            eq_t = jnp.where(eq[:, sl], 1.0, 0.0)
            rank_eq = jnp.dot(eq_t.astype(BF16), before, preferred_element_type=F32) + run_eq
            sel = jnp.logical_or(gt[:, sl], jnp.logical_and(eq[:, sl], rank_eq < need))
            sel_t = jnp.where(sel, 1.0, 0.0)
            slot = jnp.dot(sel_t.astype(BF16), before, preferred_element_type=F32) + run_sel
            pos_ref[:, start + j * LANES:start + (j + 1) * LANES] = jnp.where(sel, slot, -1.0).astype(I32)
            aff_ref[:, start + j * LANES:start + (j + 1) * LANES] = jnp.where(sel, p[:, sl], 0.0)
            run_eq = run_eq + jnp.sum(eq_t, axis=1, keepdims=True)
            run_sel = run_sel + jnp.sum(sel_t, axis=1, keepdims=True)

    block_first_slot.append(run_sel)
    lane = lax.broadcasted_iota(I32, (ne, LANES), 1)
    win0 = jnp.zeros((ne, LANES), I32)
    pages = jnp.zeros((ne, LANES), I32)
    for j in range(len(block_first_slot) - 1):
        first = block_first_slot[j].astype(I32)
        end = block_first_slot[j + 1].astype(I32)
        w0 = jnp.left_shift(jnp.right_shift(first, SLOT_ALIGN.bit_length() - 1), SLOT_ALIGN.bit_length() - 1)
        need_pages = jnp.right_shift(end - w0 + (SLOT_WIN - 1), SLOT_WIN.bit_length() - 1)
        need_pages = jnp.max(need_pages.astype(F32), axis=0, keepdims=True).astype(I32)
        win0 = jnp.where(lane == j, w0, win0)
        pages = jnp.where(lane == j, need_pages, pages)
    win_ref[...] = win0
    pages_ref[...] = pages


def _topk(probs_t, segs):
    b, ne, so = probs_t.shape
    assert so // TOK_BLOCK <= LANES
    blk = pl.BlockSpec((None, ne, so), lambda bi: (bi, 0, 0))
    tab = pl.BlockSpec((None, ne, LANES), lambda bi: (bi, 0, 0))
    pos, aff, win0, pages = pl.pallas_call(
        functools.partial(_topk_kernel, segs=segs),
        grid=(b,),
        in_specs=[blk],
        out_specs=[blk, blk, tab, tab],
        out_shape=[jax.ShapeDtypeStruct((b, ne, so), I32), jax.ShapeDtypeStruct((b, ne, so), F32),
                   jax.ShapeDtypeStruct((b, ne, LANES), I32), jax.ShapeDtypeStruct((b, ne, LANES), I32)],
        compiler_params=_cparams(("arbitrary",), VMEM_LIMIT),
        name="expert_choice_topk",
    )(probs_t)
    nblk = so // TOK_BLOCK
    win0 = jnp.swapaxes(win0[:, :, :nblk], 1, 2).reshape(-1)
    pages = pages[:, 0, :nblk].reshape(-1)
    return pos, aff, win0, pages


def _slot_windows(win_ref, base, page, n_slots):
    wins = []
    for e in range(N_EXPERTS):
        lo = win_ref[base + e] + page * SLOT_WIN
        wins.append((lo, pl.multiple_of(jnp.minimum(lo, n_slots - SLOT_WIN), SLOT_ALIGN)))
    return wins


def _window_hits(pos, wins):
    row = lax.broadcasted_iota(I32, (SLOT_WIN, TOK_BLOCK), 0)
    hits = []
    for e, (lo, start) in enumerate(wins):
        pe = pos[e:e + 1, :]
        hits.append(jnp.logical_and(pe - start == row, pe >= lo))
    return hits


def _moe_gather_kernel(win_ref, pages_ref, pos_ref, h_ref, xg_ref, *, n_slots):
    j = pl.program_id(1)
    blk = pl.program_id(0) * pl.num_programs(1) + j

    @pl.when(j == 0)
    def _():
        xg_ref[...] = jnp.zeros(xg_ref.shape, BF16)

    pos = pos_ref[...]

    def page(r, carry):
        wins = _slot_windows(win_ref, blk * N_EXPERTS, r, n_slots)
        onehot = jnp.concatenate([jnp.where(hit, 1.0, 0.0).astype(BF16) for hit in _window_hits(pos, wins)], axis=0)
        g = jnp.dot(onehot, h_ref[...], preferred_element_type=F32)
        for e, (_, start) in enumerate(wins):
            xg_ref[e, pl.ds(start, SLOT_WIN), :] += g[e * SLOT_WIN:(e + 1) * SLOT_WIN, :].astype(BF16)
        return carry

    lax.fori_loop(0, pages_ref[blk], page, 0)


def _moe_gather(win0, pages, pos, h2, n_slots):
    b, so, d = h2.shape
    ne = pos.shape[1]
    assert so % TOK_BLOCK == 0 and n_slots >= SLOT_WIN and n_slots % SLOT_ALIGN == 0 and ne == N_EXPERTS
    grid_spec = pltpu.PrefetchScalarGridSpec(
        num_scalar_prefetch=2,
        grid=(b, so // TOK_BLOCK),
        in_specs=[pl.BlockSpec((None, ne, TOK_BLOCK), lambda bi, j, w, p: (bi, 0, j)),
                  pl.BlockSpec((None, TOK_BLOCK, d), lambda bi, j, w, p: (bi, j, 0))],
        out_specs=pl.BlockSpec((None, ne, n_slots, d), lambda bi, j, w, p: (bi, 0, 0, 0)))
    return pl.pallas_call(
        functools.partial(_moe_gather_kernel, n_slots=n_slots),
        grid_spec=grid_spec,
        out_shape=jax.ShapeDtypeStruct((b, ne, n_slots, d), BF16),
        compiler_params=_cparams(("arbitrary", "arbitrary"), VMEM_LIMIT),
        name="moe_gather",
    )(win0, pages, pos, h2)


def _moe_ffn_kernel(x_ref, wg_ref, wu_ref, wd_ref, y_ref, wgb_ref, wub_ref, wdb_ref):
    @pl.when(pl.program_id(1) == 0)
    def _():
        wgb_ref[...] = wg_ref[...].astype(BF16)
        wub_ref[...] = wu_ref[...].astype(BF16)
        wdb_ref[...] = wd_ref[...].astype(BF16)

    xb = jnp.concatenate([x_ref[si] for si in range(x_ref.shape[0])], axis=0)
    gate = jnp.dot(xb, wgb_ref[...], preferred_element_type=F32)
    up = jnp.dot(xb, wub_ref[...], preferred_element_type=F32)
    hid = (_silu(gate) * up).astype(BF16)
    _put_rows(y_ref, jnp.dot(hid, wdb_ref[...], preferred_element_type=F32))


def _moe_ffn(xg, wg, wu, wd, layer):
    b, ne, n_slots, d = xg.shape
    ff = wg.shape[3]
    sb = _samples_per_step(b)
    rows = pl.BlockSpec((sb, None, n_slots, d), lambda e, bi: (bi, e, 0, 0))
    return pl.pallas_call(
        _moe_ffn_kernel,
        grid=(ne, b // sb),
        in_specs=[rows,
                  pl.BlockSpec((None, None, d, ff), lambda e, bi: (layer, e, 0, 0)),
                  pl.BlockSpec((None, None, d, ff), lambda e, bi: (layer, e, 0, 0)),
                  pl.BlockSpec((None, None, ff, d), lambda e, bi: (layer, e, 0, 0))],
        out_specs=rows,
        out_shape=jax.ShapeDtypeStruct((b, ne, n_slots, d), BF16),
        scratch_shapes=[pltpu.VMEM((d, ff), BF16), pltpu.VMEM((d, ff), BF16), pltpu.VMEM((ff, d), BF16)],
        compiler_params=_cparams(("arbitrary", "arbitrary"), VMEM_LIMIT),
        name="expert_ffn",
    )(xg, wg, wu, wd)


def _moe_combine_kernel(win_ref, pages_ref, pos_ref, aff_ref, y_ref, x_ref, mod_ref, o_ref, ycat_ref, acc_ref, *,
                        n_slots):
    blk = pl.program_id(0) * pl.num_programs(1) + pl.program_id(1)
    pos = pos_ref[...]
    aff = aff_ref[...]
    aff_hi = aff.astype(BF16).astype(F32)
    aff_lo = aff - aff_hi
    acc_ref[...] = jnp.zeros(acc_ref.shape, F32)
    tn = (((0,), (0,)), ((), ()))

    def page(r, carry):
        wins = _slot_windows(win_ref, blk * N_EXPERTS, r, n_slots)
        hits = _window_hits(pos, wins)
        for e, (_, start) in enumerate(wins):
            ycat_ref[e * SLOT_WIN:(e + 1) * SLOT_WIN, :] = y_ref[e, pl.ds(start, SLOT_WIN), :]
        w_hi = jnp.concatenate([jnp.where(hit, aff_hi[e:e + 1, :], 0.0).astype(BF16) for e, hit in enumerate(hits)],
                               axis=0)
        w_lo = jnp.concatenate([jnp.where(hit, aff_lo[e:e + 1, :], 0.0).astype(BF16) for e, hit in enumerate(hits)],
                               axis=0)
        ycat = ycat_ref[...]
        acc_ref[...] += (lax.dot_general(w_hi, ycat, tn, preferred_element_type=F32)
                         + lax.dot_general(w_lo, ycat, tn, preferred_element_type=F32))
        return carry

    lax.fori_loop(0, pages_ref[blk], page, 0)
    o_ref[...] = x_ref[...] + mod_ref[5:6, :] * acc_ref[...]


def _moe_combine(win0, pages, pos, aff, y, xmid, mod8, tile_off, n_ctx_tiles):
    b, so, d = xmid.shape
    _, ne, n_slots, _ = y.shape
    assert TOK_BLOCK == TOK_TILE
    if n_ctx_tiles and tile_off < n_ctx_tiles:
        seg = lambda i: jnp.minimum((i + tile_off) // n_ctx_tiles, 1)
    else:
        seg = lambda i: 1
    sel = pl.BlockSpec((None, ne, TOK_BLOCK), lambda bi, j, w, p: (bi, 0, j))
    tok = pl.BlockSpec((None, TOK_BLOCK, d), lambda bi, j, w, p: (bi, j, 0))
    grid_spec = pltpu.PrefetchScalarGridSpec(
        num_scalar_prefetch=2,
        grid=(b, so // TOK_BLOCK),
        in_specs=[sel, sel,
                  pl.BlockSpec((None, ne, n_slots, d), lambda bi, j, w, p: (bi, 0, 0, 0)),
                  tok,
                  pl.BlockSpec((None, None, 8, d), lambda bi, j, w, p: (bi, seg(j), 0, 0))],
        out_specs=tok,
        scratch_shapes=[pltpu.VMEM((ne * SLOT_WIN, d), BF16), pltpu.VMEM((TOK_BLOCK, d), F32)])
    return pl.pallas_call(
        functools.partial(_moe_combine_kernel, n_slots=n_slots),
        grid_spec=grid_spec,
        out_shape=jax.ShapeDtypeStruct((b, so, d), F32),
        compiler_params=_cparams(("arbitrary", "arbitrary"), VMEM_LIMIT),
        name="moe_combine",
    )(win0, pages, pos, aff, y, xmid, mod8)


def _rope_tables(ctx_len, lat_len):
    rows = lat_len // GRID_W
    row = jnp.repeat(jnp.arange(rows), GRID_W).astype(F32)
    col = jnp.tile(jnp.arange(GRID_W), rows).astype(F32)
    axis_dim = ATT_QK // 2
    inv = 1.0 / (ROPE_BASE ** (jnp.arange(0, axis_dim, 2, dtype=F32) / axis_dim))
    ang_row = row[:, None] * inv
    ang_col = col[:, None] * inv
    z = jnp.zeros_like(ang_row)
    cos64 = jnp.concatenate([jnp.cos(ang_row), jnp.cos(ang_row), jnp.cos(ang_col), jnp.cos(ang_col)], axis=-1)
    up64 = jnp.concatenate([-jnp.sin(ang_row), z, -jnp.sin(ang_col), z], axis=-1)
    dn64 = jnp.concatenate([z, jnp.sin(ang_row), z, jnp.sin(ang_col)], axis=-1)
    lat = jnp.stack([jnp.tile(t, (1, 2)) for t in (cos64, up64, dn64)])
    ctx = jnp.stack([jnp.ones((ctx_len, LANES), F32), jnp.zeros((ctx_len, LANES), F32),
                     jnp.zeros((ctx_len, LANES), F32)])
    return jnp.concatenate([ctx, lat], axis=1)


def kernel(x, c, ctx, c_ctx, mod_w, mod_b, norm1_g, norm2_g, w_in, w_out, att_q_norm_g, att_k_norm_g, att_lambda,
           att_subln_g, ret_log_decay, ret_norm_g, lru_conv_w, lru_conv_b, lru_gate_w, lru_gate_b, lru_lambda,
           lru_norm_g, router_w, exp_w_gate, exp_w_up, exp_w_down):
    b, lat_len, d = x.shape
    ctx_len = ctx.shape[1]
    depth = mod_w.shape[0]
    s = ctx_len + lat_len
    assert ctx_len % TOK_TILE == 0 and lat_len % KEY_BLOCK == 0 and lat_len % GRID_W == 0
    n_ctx_tiles = ctx_len // TOK_TILE

    rows = -(-(b + 1) // 8) * 8
    cc = jnp.zeros((rows, d), F32).at[:b].set(c).at[b].set(c_ctx)
    mod_all = _modulation(cc, mod_w, mod_b).reshape(depth, rows, 6, d)

    rope = _rope_tables(ctx_len, lat_len)
    lane = np.arange(2 * LANES)
    bd = jnp.asarray((lane[:, None] // 64) == (lane[None, :] // 64), BF16)
    ctx_src, lat_src = ctx, x

    out = None
    for i in range(depth):
        need_ctx = i < depth - 1
        m6 = mod_all[i]
        mod8 = jnp.stack([jnp.broadcast_to(m6[b], (b, 6, d)), m6[:b]], axis=1)
        mod8 = jnp.pad(mod8, ((0, 0), (0, 0), (0, 2), (0, 0)))
        gqk = jnp.stack([jnp.tile(att_q_norm_g[i], 2), jnp.tile(att_k_norm_g[i], 2)])
        q, kt, v, rq, rk, rv, gr, xu, gu = _inproj(ctx_src, lat_src, s, mod8, norm1_g[i][None],
                                                   w_in[i].astype(BF16), rope, gqk, bd, n_ctx_tiles)
        tile_off = 0 if need_ctx else n_ctx_tiles
        tok_off = tile_off * TOK_TILE
        lam_init = 0.8 - 0.6 * math.exp(-0.3 * i)
        lam_p = jnp.pad(att_lambda[i], ((0, 0), (0, LANES - ATT_QK)))
        att = _attention(lam_p, q, kt, v, att_subln_g[i][None], ctx_len, tile_off, n_ctx_tiles, lam_init)
        ret = _retention(ret_log_decay[i], rq, rk, rv, gr, jnp.tile(ret_norm_g[i], RET_HEADS)[None], bd, ctx_len,
                         tok_off)
        gw = lru_gate_w[i]
        gate_wd = jnp.zeros((2, 2, LRU_W, LRU_W), F32)
        bw = gw.shape[-1]
        for n in range(gw.shape[2]):
            gate_wd = gate_wd.at[:, :, n * bw:(n + 1) * bw, n * bw:(n + 1) * bw].set(gw[:, :, n])
        lru = _rglru(xu, gu, lru_conv_w[i], lru_conv_b[i][None], gate_wd.astype(BF16),
                     lru_gate_b[i].reshape(4, LRU_W), lru_lambda[i], lru_norm_g[i][None], ctx_len, tok_off)
        rw = jnp.pad(router_w[i], ((0, 0), (0, LANES - N_EXPERTS)))
        rw_hi = rw.astype(BF16)
        router2 = jnp.stack([rw_hi, (rw - rw_hi.astype(F32)).astype(BF16)])
        xmid, h2, probs_t = _outproj(att, ret, lru, ctx_src, lat_src, mod8, norm2_g[i][None],
                                     w_out[i].astype(BF16), router2, tile_off, n_ctx_tiles)
        segs = []
        if need_ctx:
            segs.append((0, ctx_len, EC_CAPACITY * ctx_len // N_EXPERTS))
        segs.append((ctx_len - tok_off, lat_len, EC_CAPACITY * lat_len // N_EXPERTS))
        pos, aff, win0, pages = _topk(probs_t, tuple(segs))
        xg = _moe_gather(win0, pages, pos, h2, sum(k for _, _, k in segs))
        y = _moe_ffn(xg, exp_w_gate, exp_w_up, exp_w_down, i)
        out = _moe_combine(win0, pages, pos, aff, y, xmid, mod8, tile_off, n_ctx_tiles)
        ctx_src = lat_src = out
    return out
```

```python
import functools
import math

import jax
import jax.numpy as jnp
import numpy as np
from jax import lax
from jax.experimental import pallas as pl
from jax.experimental.pallas import tpu as pltpu

F32 = jnp.float32
BF16 = jnp.bfloat16
I32 = jnp.int32

ATT_HEADS = 4
ATT_QK = 64
ATT_V = 2 * ATT_QK
ATT_W = ATT_HEADS * ATT_V
RET_HEADS = 4
RET_QK = 64
RET_W = 256
LRU_W = 256
LRU_C = 8.0
N_EXPERTS = 16
EC_CAPACITY = 2
EPS = 1e-6
ROPE_BASE = 10000.0
GRID_W = 64
IN_COLS = 3072
Q_SCALE = ATT_QK ** -0.5 * math.log2(math.e)

LANES = 128
SUBLANES = 8
TOK_TILE = 256
KEY_BLOCK = 512
CHUNK = 128
RET_CHUNK = 256
TOK_BLOCK = 256
SLOT_WIN = 64
SLOT_ALIGN = 16
VMEM_LIMIT = 56 * 1024 * 1024


def _cparams(sem, vmem=None):
    return pltpu.CompilerParams(dimension_semantics=sem, vmem_limit_bytes=vmem)


def _sigmoid(x):
    return 0.5 * (jnp.tanh(0.5 * x) + 1.0)


def _silu(x):
    return x * _sigmoid(x)


def _group_mean_sq(t, bd):
    sq = t * t
    hi = sq.astype(BF16)
    lo = (sq - hi.astype(F32)).astype(BF16)
    ss = jnp.dot(hi, bd, preferred_element_type=F32) + jnp.dot(lo, bd, preferred_element_type=F32)
    return ss * (1.0 / 64.0)


def _mod_kernel(c_ref, w_ref, b_ref, o_ref):
    s = _silu(c_ref[...]).astype(BF16)
    o_ref[...] = jnp.dot(s, w_ref[...].astype(BF16), preferred_element_type=F32) + b_ref[...]


def _modulation(cc, mod_w, mod_b):
    depth, d, d6 = mod_w.shape
    rows = cc.shape[0]
    return pl.pallas_call(
        _mod_kernel,
        grid=(depth, d6 // d),
        in_specs=[pl.BlockSpec((rows, d), lambda l, j: (0, 0)),
                  pl.BlockSpec((None, d, d), lambda l, j: (l, 0, j)),
                  pl.BlockSpec((None, 1, d), lambda l, j: (l, 0, j))],
        out_specs=pl.BlockSpec((None, rows, d), lambda l, j: (l, 0, j)),
        out_shape=jax.ShapeDtypeStruct((depth, rows, d6), F32),
        compiler_params=_cparams(("arbitrary", "arbitrary")),
        name="modulation",
    )(cc, mod_w, mod_b.reshape(depth, 1, d6))


def _samples_per_step(b):
    return 2 if b % 2 == 0 else 1


def _token_sources(ctx_src, lat_src, tile_off, n_ctx_tiles, d, sb):
    lat_base = n_ctx_tiles if lat_src is ctx_src else 0
    ctx_spec = pl.BlockSpec((sb, TOK_TILE, d), lambda bi, i, *_: (bi, jnp.minimum(i + tile_off, n_ctx_tiles - 1), 0))
    lat_spec = pl.BlockSpec((sb, TOK_TILE, d),
                            lambda bi, i, *_: (bi, jnp.maximum(i + tile_off - n_ctx_tiles, 0) + lat_base, 0))
    return ctx_spec, lat_spec


def _put_rows(ref, val, cols=slice(None)):
    rows = ref.shape[1]
    for si in range(ref.shape[0]):
        ref[si, :, cols] = val[si * rows:(si + 1) * rows].astype(ref.dtype)


def _inproj_kernel(xc_ref, xl_ref, mod_ref, ng_ref, w_ref, rope_ref, gqk_ref, bd_ref,
                   q_ref, kt_ref, v_ref, rq_ref, rk_ref, rv_ref, gr_ref, xu_ref, gu_ref, *, n_ctx_tiles):
    sb, tm, _ = xc_ref.shape
    is_ctx = pl.program_id(1) < n_ctx_tiles
    hs = []
    for si in range(sb):
        x = jnp.where(is_ctx, xc_ref[si], xl_ref[si])
        ms = jnp.mean(x * x, axis=-1, keepdims=True)
        y = x * lax.rsqrt(ms + EPS) * ng_ref[...]
        hs.append((y * (1.0 + mod_ref[si, 1:2, :]) + mod_ref[si, 0:1, :]).astype(BF16))
    hb = jnp.concatenate(hs, axis=0)

    def proj(c0, c1):
        return jnp.dot(hb, w_ref[:, c0:c1], preferred_element_type=F32)

    bd = bd_ref[...]
    cosr = rope_ref[0]
    sin_up = rope_ref[1]
    sin_dn = rope_ref[2]

    def norm_rope(t, ms, g):
        tn = t * lax.rsqrt(ms + EPS) * g
        return tn * cosr + pltpu.roll(tn, LANES - 16, 1) * sin_up + pltpu.roll(tn, 16, 1) * sin_dn

    tqk = proj(0, 2 * ATT_W)
    for hd in range(ATT_HEADS):
        c0 = hd * ATT_V
        tq = tqk[:, c0:c0 + ATT_V]
        tk = tqk[:, ATT_W + c0:ATT_W + c0 + ATT_V]
        ms = _group_mean_sq(jnp.concatenate([tq, tk], axis=1), bd)
        for si in range(sb):
            rows = slice(si * tm, (si + 1) * tm)
            q_ref[si, :, c0:c0 + ATT_V] = (norm_rope(tq[rows], ms[rows, :ATT_V], gqk_ref[0:1, :])
                                           * Q_SCALE).astype(BF16)
            kt_ref[si, c0:c0 + ATT_V, :] = norm_rope(tk[rows], ms[rows, ATT_V:], gqk_ref[1:2, :]).T.astype(BF16)
    _put_rows(v_ref, proj(2 * ATT_W, 3 * ATT_W))
    base = 3 * ATT_W
    _put_rows(rq_ref, proj(base, base + RET_W))
    _put_rows(rk_ref, proj(base + RET_W, base + 2 * RET_W) * (RET_QK ** -0.5))
    _put_rows(rv_ref, proj(base + 2 * RET_W, base + 3 * RET_W))
    _put_rows(gr_ref, proj(base + 3 * RET_W, base + 4 * RET_W))
    _put_rows(xu_ref, proj(base + 4 * RET_W, base + 4 * RET_W + LRU_W))
    _put_rows(gu_ref, proj(base + 4 * RET_W + LRU_W, base + 4 * RET_W + 2 * LRU_W))


def _inproj(ctx_src, lat_src, s, mod8, norm_g, w_in_b, rope, gqk, bd, n_ctx_tiles):
    b, _, d = ctx_src.shape
    tm = TOK_TILE
    sb = _samples_per_step(b)
    seg = lambda i: jnp.minimum(i // n_ctx_tiles, 1)
    tok = lambda w: pl.BlockSpec((sb, tm, w), lambda bi, i: (bi, i, 0))
    f32o = lambda w: jax.ShapeDtypeStruct((b, s, w), F32)
    ctx_spec, lat_spec = _token_sources(ctx_src, lat_src, 0, n_ctx_tiles, d, sb)
    return pl.pallas_call(
        functools.partial(_inproj_kernel, n_ctx_tiles=n_ctx_tiles),
        grid=(b // sb, s // tm),
        in_specs=[ctx_spec, lat_spec,
                  pl.BlockSpec((sb, None, 8, d), lambda bi, i: (bi, seg(i), 0, 0)),
                  pl.BlockSpec((1, d), lambda bi, i: (0, 0)),
                  pl.BlockSpec((d, IN_COLS), lambda bi, i: (0, 0)),
                  pl.BlockSpec((3, tm, LANES), lambda bi, i: (0, i, 0)),
                  pl.BlockSpec((2, LANES), lambda bi, i: (0, 0)),
                  pl.BlockSpec((2 * LANES, 2 * LANES), lambda bi, i: (0, 0))],
        out_specs=[tok(ATT_W),
                   pl.BlockSpec((sb, ATT_W, tm), lambda bi, i: (bi, 0, i)),
                   tok(ATT_W), tok(RET_W), tok(RET_W), tok(RET_W), tok(RET_W), tok(LRU_W), tok(LRU_W)],
        out_shape=[jax.ShapeDtypeStruct((b, s, ATT_W), BF16),
                   jax.ShapeDtypeStruct((b, ATT_W, s), BF16),
                   jax.ShapeDtypeStruct((b, s, ATT_W), BF16),
                   f32o(RET_W), f32o(RET_W), f32o(RET_W), f32o(RET_W), f32o(LRU_W), f32o(LRU_W)],
        compiler_params=_cparams(("arbitrary", "arbitrary"), VMEM_LIMIT),
        name="inproj",
    )(ctx_src, lat_src, mod8, norm_g, w_in_b, rope, gqk, bd)


def _attn_kernel(lam_ref, q_ref, kt_ref, v_ref, g_ref, o_ref, *, ctx_len, q_off, n_ctx_tiles, lam_init):
    seq_len = kt_ref.shape[1]
    lp = lam_ref[...]
    lam = (jnp.exp(jnp.sum(lp[0:1, :] * lp[1:2, :], axis=1, keepdims=True))
           - jnp.exp(jnp.sum(lp[2:3, :] * lp[3:4, :], axis=1, keepdims=True)) + lam_init)
    q = q_ref[...]
    lane = lax.broadcasted_iota(I32, q.shape, 1)

    def attend(n_keys):
        outs = []
        for mi in range(2):
            keep = (lane < ATT_QK) if mi == 0 else (lane >= ATT_QK)
            qm = jnp.where(keep, q, jnp.zeros_like(q))
            s = jnp.dot(qm, kt_ref[:, 0:n_keys], preferred_element_type=F32)
            p = jnp.exp2(s - jnp.max(s, axis=1, keepdims=True))
            outs.append((p, jnp.sum(p, axis=1, keepdims=True)))
        a = outs[0][0] * (1.0 / outs[0][1]) - outs[1][0] * (lam / outs[1][1])
        o = jnp.dot(a.astype(BF16), v_ref[0:n_keys, :], preferred_element_type=F32)
        o = o * lax.rsqrt(jnp.mean(o * o, axis=-1, keepdims=True) + EPS) * g_ref[...] * (1.0 - lam_init)
        o_ref[...] = o.astype(BF16)

    if q_off >= n_ctx_tiles:
        attend(seq_len)
    else:
        is_ctx = pl.program_id(2) + q_off < n_ctx_tiles
        pl.when(is_ctx)(lambda: attend(ctx_len))
        pl.when(jnp.logical_not(is_ctx))(lambda: attend(seq_len))


def _attention(lam_p, q, kt, v, subln_g, ctx_len, q_off, n_ctx_tiles, lam_init):
    b, s, _ = q.shape
    tq = TOK_TILE
    nq = s // tq - q_off
    kern = functools.partial(_attn_kernel, ctx_len=ctx_len, q_off=q_off, n_ctx_tiles=n_ctx_tiles,
                             lam_init=lam_init)
    return pl.pallas_call(
        kern,
        grid=(b, ATT_HEADS, nq),
        in_specs=[pl.BlockSpec((4, LANES), lambda bi, h, i: (0, 0)),
                  pl.BlockSpec((None, tq, ATT_V), lambda bi, h, i: (bi, i + q_off, h)),
                  pl.BlockSpec((None, ATT_V, s), lambda bi, h, i: (bi, h, 0)),
                  pl.BlockSpec((None, s, ATT_V), lambda bi, h, i: (bi, 0, h)),
                  pl.BlockSpec((1, ATT_V), lambda bi, h, i: (0, 0))],
        out_specs=pl.BlockSpec((None, tq, ATT_V), lambda bi, h, i: (bi, i, h)),
        out_shape=jax.ShapeDtypeStruct((b, nq * tq, ATT_W), BF16),
        compiler_params=_cparams(("arbitrary", "arbitrary", "arbitrary"), VMEM_LIMIT),
        name="diff_attention",
    )(lam_p, q, kt, v, subln_g)


def _ret_kernel(lg_ref, q_ref, k_ref, v_ref, gr_ref, g_ref, bd_ref, o_ref,
                of_ref, ob_ref, st_ref, dm_ref, cr_ref, in_ref, cd_ref, *, n_ctx_chunks, n_chunks, out_chunk0):
    c = RET_CHUNK
    rown = lax.broadcasted_iota(I32, (c, c), 0)
    colm = lax.broadcasted_iota(I32, (c, c), 1)
    low = lax.broadcasted_iota(I32, (c, LANES), 1) < RET_QK
    rowf = lax.broadcasted_iota(I32, (c, LANES), 0).astype(F32)
    srow = lax.broadcasted_iota(I32, (LANES, LANES), 0)
    scol = lax.broadcasted_iota(I32, (LANES, LANES), 1)
    for d in range(2):
        if d == 0:
            diff = (rown - colm).astype(F32)
            mask = rown >= colm
            cross_pw = rowf + 1.0
            inner_pw = (c - 1.0) - rowf
        else:
            diff = (colm - rown).astype(F32)
            mask = colm > rown
            cross_pw = c - rowf
            inner_pw = rowf
        for hp in range(2):
            dm_ref[d, hp] = jnp.concatenate(
                [jnp.where(mask, jnp.exp(lg_ref[d, 2 * hp + j] * jnp.where(mask, diff, 0.0)), 0.0) for j in range(2)],
                axis=1)
            lg_lane = jnp.where(low, lg_ref[d, 2 * hp], lg_ref[d, 2 * hp + 1])
            cr_ref[d, hp] = jnp.exp(lg_lane * cross_pw)
            in_ref[d, hp] = jnp.exp(lg_lane * inner_pw)
            cd_ref[d, hp] = jnp.exp(jnp.where(srow < RET_QK, lg_ref[d, 2 * hp], lg_ref[d, 2 * hp + 1]) * float(c))
    st_ref[...] = jnp.zeros(st_ref.shape, F32)
    bdmask = (srow < RET_QK) == (scol < RET_QK)

    def step(i, carry):
        for d in range(2):
            if d == 0:
                ci = i
            else:
                ci = jnp.where(i < n_ctx_chunks, n_ctx_chunks - 1 - i, n_chunks - 1 - (i - n_ctx_chunks))
            r0 = pl.multiple_of(ci * c, c)
            dst = of_ref if d == 0 else ob_ref
            for hp in range(2):
                cols = slice(hp * LANES, (hp + 1) * LANES)
                kf = k_ref[pl.ds(r0, c), cols]
                qf = q_ref[pl.ds(r0, c), cols]
                kb = kf.astype(BF16)
                vb = v_ref[pl.ds(r0, c), cols].astype(BF16)
                zero = jnp.zeros_like(kb)
                nt = (((1,), (1,)), ((), ()))
                kcat = jnp.concatenate([jnp.where(low, kb, zero), jnp.where(low, zero, kb)], axis=0)
                sc = (lax.dot_general(qf.astype(BF16), kcat, nt, preferred_element_type=F32)
                      * dm_ref[d, hp]).astype(BF16)
                vbd = jnp.concatenate([jnp.where(low, vb, zero), jnp.where(low, zero, vb)], axis=0)
                st = st_ref[d, hp]
                intra = jnp.dot(sc, vbd, preferred_element_type=F32)
                inter = jnp.dot(qf.astype(BF16), st.astype(BF16), preferred_element_type=F32) * cr_ref[d, hp]
                dst[pl.ds(r0, c), cols] = intra + inter
                kin = (kf * in_ref[d, hp]).astype(BF16)
                kv = lax.dot_general(kin, vb, (((0,), (0,)), ((), ())), preferred_element_type=F32)
                st_ref[d, hp] = st * cd_ref[d, hp] + jnp.where(bdmask, kv, 0.0)
        return carry

    lax.fori_loop(0, n_chunks, step, 0, unroll=2)

    bd = bd_ref[...]

    def finish(i, carry):
        r0 = pl.multiple_of((i + out_chunk0) * c, c)
        ro = pl.multiple_of(i * c, c)
        o = of_ref[pl.ds(r0, c), :] + ob_ref[pl.ds(r0, c), :]
        on = o * lax.rsqrt(_group_mean_sq(o, bd) + EPS) * g_ref[...]
        o_ref[pl.ds(ro, c), :] = (on * _silu(gr_ref[pl.ds(r0, c), :])).astype(BF16)
        return carry

    lax.fori_loop(0, n_chunks - out_chunk0, finish, 0)


def _retention(log_decay, rq, rk, rv, gr, norm_g, bd, ctx_len, out_off):
    b, s, w = rq.shape
    c = RET_CHUNK
    assert ctx_len % c == 0 and s % c == 0 and out_off % c == 0 and w == 2 * LANES
    n_chunks = s // c
    out_chunk0 = out_off // c
    kern = functools.partial(_ret_kernel, n_ctx_chunks=ctx_len // c, n_chunks=n_chunks, out_chunk0=out_chunk0)
    seq = pl.BlockSpec((None, s, w), lambda bi: (bi, 0, 0))
    state = pltpu.VMEM((2, 2, LANES, LANES), F32)
    rows = pltpu.VMEM((2, 2, c, LANES), F32)
    return pl.pallas_call(
        kern,
        grid=(b,),
        in_specs=[pl.BlockSpec(memory_space=pltpu.SMEM), seq, seq, seq, seq,
                  pl.BlockSpec((1, w), lambda bi: (0, 0)),
                  pl.BlockSpec((2 * LANES, 2 * LANES), lambda bi: (0, 0))],
        out_specs=pl.BlockSpec((None, s - out_off, w), lambda bi: (bi, 0, 0)),
        out_shape=jax.ShapeDtypeStruct((b, s - out_off, w), BF16),
        scratch_shapes=[pltpu.VMEM((s, w), F32), pltpu.VMEM((s, w), F32), state,
                        pltpu.VMEM((2, 2, c, 2 * c), F32), rows, rows, state],
        compiler_params=_cparams(("arbitrary",), VMEM_LIMIT),
        name="retention",
    )(log_decay, rq, rk, rv, gr, norm_g, bd)


def _gelu_tanh(x):
    return 0.5 * x * (1.0 + jnp.tanh(math.sqrt(2.0 / math.pi) * (x + 0.044715 * (x * x * x))))


def _lru_kernel(xu_ref, gu_ref, cw_ref, cb_ref, gw_ref, gb_ref, lam_ref, ng_ref, o_ref,
                u_ref, hf_ref, hb_ref, *, ctx_len, seq_len, out_chunk0):
    c = CHUNK
    n_chunks = seq_len // c
    n_ctx_chunks = ctx_len // c
    row = lax.broadcasted_iota(I32, (c, LRU_W), 0)

    def conv(i, carry):
        r0 = pl.multiple_of(i * c, c)
        prev0 = pl.multiple_of(jnp.maximum(r0 - 8, 0), 8)
        next0 = pl.multiple_of(jnp.minimum(r0 + c, seq_len - 8), 8)
        ext = jnp.concatenate([xu_ref[pl.ds(prev0, 8), :], xu_ref[pl.ds(r0, c), :], xu_ref[pl.ds(next0, 8), :]],
                              axis=0)
        seg_first = jnp.logical_or(i == 0, i == n_ctx_chunks)
        seg_last = jnp.logical_or(i == n_ctx_chunks - 1, i == n_chunks - 1)
        xm2 = pltpu.roll(ext, 2, 0)[8:8 + c]
        xm1 = pltpu.roll(ext, 1, 0)[8:8 + c]
        xp1 = pltpu.roll(ext, c + 16 - 1, 0)[8:8 + c]
        xm2 = jnp.where(jnp.logical_and(seg_first, row < 2), 0.0, xm2)
        xm1 = jnp.where(jnp.logical_and(seg_first, row < 1), 0.0, xm1)
        xp1 = jnp.where(jnp.logical_and(seg_last, row >= c - 1), 0.0, xp1)
        u_ref[pl.ds(r0, c), :] = (cw_ref[0:1, :] * xm2 + cw_ref[1:2, :] * xm1 + cw_ref[2:3, :] * ext[8:8 + c]
                                  + cw_ref[3:4, :] * xp1 + cb_ref[...])
        return carry

    lax.fori_loop(0, n_chunks, conv, 0)

    def softplus(z):
        return jnp.maximum(z, 0.0) + jnp.log1p(jnp.exp(-jnp.abs(z)))

    def scan_dir(d, u):
        ub = u.astype(BF16)
        zr = jnp.dot(ub, gw_ref[d, 0], preferred_element_type=F32) + gb_ref[2 * d:2 * d + 1, :]
        zi = jnp.dot(ub, gw_ref[d, 1], preferred_element_type=F32) + gb_ref[2 * d + 1:2 * d + 2, :]
        log_a = -LRU_C * softplus(-lam_ref[d:d + 1, :]) * _sigmoid(zr)
        a = jnp.exp(log_a)
        b = jnp.sqrt(-jnp.tanh(log_a) * (a * a + 1.0)) * (_sigmoid(zi) * u)
        ng = c // SUBLANES
        a = a.reshape(ng, SUBLANES, LRU_W)
        b = b.reshape(ng, SUBLANES, LRU_W)
        sub = lax.broadcasted_iota(I32, (ng, SUBLANES, LRU_W), 1)
        sh = 1
        while sh < SUBLANES:
            if d == 0:
                valid = sub >= sh
                a_prev = pltpu.roll(a, sh, 1)
                b_prev = pltpu.roll(b, sh, 1)
            else:
                valid = sub < SUBLANES - sh
                a_prev = pltpu.roll(a, SUBLANES - sh, 1)
                b_prev = pltpu.roll(b, SUBLANES - sh, 1)
            b = a * jnp.where(valid, b_prev, 0.0) + b
            a = a * jnp.where(valid, a_prev, 1.0)
            sh *= 2
        return a, b

    def chain(d, a, b, h):
        ng = c // SUBLANES
        out = [None] * ng
        for g in (range(ng) if d == 0 else range(ng - 1, -1, -1)):
            hg = a[g] * h + b[g]
            out[g] = hg
            h = hg[SUBLANES - 1:SUBLANES, :] if d == 0 else hg[0:1, :]
        return jnp.concatenate(out, axis=0), h

    def step(i, carry):
        h_f, h_b = carry
        r0 = pl.multiple_of(i * c, c)
        a, b = scan_dir(0, u_ref[pl.ds(r0, c), :])
        hh, h_f = chain(0, a, b, h_f)
        hf_ref[pl.ds(r0, c), :] = hh
        cb = jnp.where(i < n_ctx_chunks, n_ctx_chunks - 1 - i, n_chunks - 1 - (i - n_ctx_chunks))
        rb = pl.multiple_of(cb * c, c)
        a, b = scan_dir(1, u_ref[pl.ds(rb, c), :])
        hh, h_b = chain(1, a, b, h_b)
        hb_ref[pl.ds(rb, c), :] = hh
        return h_f, h_b

    zero = jnp.zeros((1, LRU_W), F32)
    lax.fori_loop(0, n_chunks, step, (zero, zero))

    def finish(i, carry):
        r0 = pl.multiple_of((i + out_chunk0) * c, c)
        ro = pl.multiple_of(i * c, c)
        t = (hf_ref[pl.ds(r0, c), :] + hb_ref[pl.ds(r0, c), :]) * _gelu_tanh(gu_ref[pl.ds(r0, c), :])
        t = t * lax.rsqrt(jnp.mean(t * t, axis=-1, keepdims=True) + EPS) * ng_ref[...]
        o_ref[pl.ds(ro, c), :] = t.astype(BF16)
        return carry

    lax.fori_loop(0, n_chunks - out_chunk0, finish, 0)


def _rglru(xu, gu, conv_w, conv_b, gate_wd, gate_b, lam, norm_g, ctx_len, out_off):
    b, s, w = xu.shape
    kern = functools.partial(_lru_kernel, ctx_len=ctx_len, seq_len=s, out_chunk0=out_off // CHUNK)
    seq = pl.BlockSpec((None, s, w), lambda bi: (bi, 0, 0))
    full = lambda shp: pl.BlockSpec(shp, lambda bi: (0,) * len(shp))
    return pl.pallas_call(
        kern,
        grid=(b,),
        in_specs=[seq, seq, full((4, w)), full((1, w)), full((2, 2, w, w)), full((4, w)), full((2, w)),
                  full((1, w))],
        out_specs=pl.BlockSpec((None, s - out_off, w), lambda bi: (bi, 0, 0)),
        out_shape=jax.ShapeDtypeStruct((b, s - out_off, w), BF16),
        scratch_shapes=[pltpu.VMEM((s, w), F32), pltpu.VMEM((s, w), F32), pltpu.VMEM((s, w), F32)],
        compiler_params=_cparams(("arbitrary",), VMEM_LIMIT),
        name="rglru",
    )(xu, gu, conv_w, conv_b, gate_wd, gate_b, lam, norm_g)


def _outproj_kernel(att_ref, ret_ref, lru_ref, xc_ref, xl_ref, mod_ref, ng_ref, w_ref, rw_ref,
                    xmid_ref, h2_ref, pt_ref, *, tile_off, n_ctx_tiles):
    sb, tm, _ = att_ref.shape
    mix = jnp.concatenate([jnp.concatenate([att_ref[si], ret_ref[si], lru_ref[si]], axis=1) for si in range(sb)],
                          axis=0)
    y = jnp.dot(mix, w_ref[...], preferred_element_type=F32)
    is_ctx = pl.program_id(1) + tile_off < n_ctx_tiles
    hs = []
    for si in range(sb):
        x = jnp.where(is_ctx, xc_ref[si], xl_ref[si]) + mod_ref[si, 2:3, :] * y[si * tm:(si + 1) * tm]
        xmid_ref[si] = x
        ms = jnp.mean(x * x, axis=-1, keepdims=True)
        hs.append((x * lax.rsqrt(ms + EPS) * ng_ref[...]) * (1.0 + mod_ref[si, 4:5, :]) + mod_ref[si, 3:4, :])
    h = jnp.concatenate(hs, axis=0)
    hb = h.astype(BF16)
    _put_rows(h2_ref, hb)
    hlo = (h - hb.astype(F32)).astype(BF16)
    wide = jnp.dot(hb, rw_ref[...], preferred_element_type=F32)
    logits = (wide[:, :LANES] + wide[:, LANES:]
              + jnp.dot(hlo, rw_ref[:, :LANES], preferred_element_type=F32))
    lane = lax.broadcasted_iota(I32, logits.shape, 1)
    logits = jnp.where(lane < N_EXPERTS, logits, -jnp.inf)
    e = jnp.exp(logits - jnp.max(logits, axis=-1, keepdims=True))
    probs = e / jnp.sum(e, axis=-1, keepdims=True)
    for si in range(sb):
        pt_ref[si] = probs[si * tm:(si + 1) * tm].T[0:N_EXPERTS, :]


def _outproj(att, ret, lru, ctx_src, lat_src, mod8, norm_g, w_out_b, router2, tile_off, n_ctx_tiles):
    b, so, _ = att.shape
    d = ctx_src.shape[-1]
    tm = TOK_TILE
    sb = 1
    seg = lambda i: jnp.minimum((i + tile_off) // n_ctx_tiles, 1)
    tok = lambda w: pl.BlockSpec((sb, tm, w), lambda bi, i: (bi, i, 0))
    ctx_spec, lat_spec = _token_sources(ctx_src, lat_src, tile_off, n_ctx_tiles, d, sb)
    return pl.pallas_call(
        functools.partial(_outproj_kernel, tile_off=tile_off, n_ctx_tiles=n_ctx_tiles),
        grid=(b // sb, so // tm),
        in_specs=[tok(ATT_W), tok(RET_W), tok(LRU_W), ctx_spec, lat_spec,
                  pl.BlockSpec((sb, None, 8, d), lambda bi, i: (bi, seg(i), 0, 0)),
                  pl.BlockSpec((1, d), lambda bi, i: (0, 0)),
                  pl.BlockSpec((d, d), lambda bi, i: (0, 0)),
                  pl.BlockSpec((d, 2 * LANES), lambda bi, i: (0, 0))],
        out_specs=[tok(d), tok(d), pl.BlockSpec((sb, N_EXPERTS, tm), lambda bi, i: (bi, 0, i))],
        out_shape=[jax.ShapeDtypeStruct((b, so, d), F32), jax.ShapeDtypeStruct((b, so, d), BF16),
                   jax.ShapeDtypeStruct((b, N_EXPERTS, so), F32)],
        compiler_params=_cparams(("arbitrary", "arbitrary"), VMEM_LIMIT),
        name="outproj_router",
    )(att, ret, lru, ctx_src, lat_src, mod8, norm_g, w_out_b, router2)


def _topk_kernel(p_ref, pos_ref, aff_ref, win_ref, pages_ref, *, segs):
    ne = p_ref.shape[0]
    block_first_slot = []
    ri = lax.broadcasted_iota(I32, (LANES, LANES), 0)
    ci = lax.broadcasted_iota(I32, (LANES, LANES), 1)
    before = jnp.where(ri < ci, 1.0, 0.0).astype(BF16)
    run_sel = jnp.zeros((ne, 1), F32)

    def kth_floor(vals, k):
        t = jnp.zeros((ne, 1), I32)
        for bit in range(30, -1, -1):
            cand = t | (1 << bit)
            cnt = jnp.sum(jnp.where(vals >= pltpu.bitcast(cand, F32), 1.0, 0.0), axis=1, keepdims=True)
            t = jnp.where(cnt >= k, cand, t)
        return pltpu.bitcast(t, F32)

    for start, ln, k in segs:
        p = p_ref[:, start:start + ln]
        resid = p - kth_floor(p, k)
        t = kth_floor(resid, k)
        gt = resid > t
        eq = resid == t
        need = k - jnp.sum(jnp.where(gt, 1.0, 0.0), axis=1, keepdims=True)
        run_eq = jnp.zeros((ne, 1), F32)
        for j in range(ln // LANES):
            sl = slice(j * LANES, (j + 1) * LANES)
            if (start + j * LANES) % TOK_BLOCK == 0:
                block_first_slot.append(run_sel)
            eq_t = jnp.where(eq[:, sl], 1.0, 0.0)
            rank_eq = jnp.dot(eq_t.astype(BF16), before, preferred_element_type=F32) + run_eq
            sel = jnp.logical_or(gt[:, sl], jnp.logical_and(eq[:, sl], rank_eq < need))
            sel_t = jnp.where(sel, 1.0, 0.0)
            slot = jnp.dot(sel_t.astype(BF16), before, preferred_element_type=F32) + run_sel
            pos_ref[:, start + j * LANES:start + (j + 1) * LANES] = jnp.where(sel, slot, -1.0).astype(I32)
            aff_ref[:, start + j * LANES:start + (j + 1) * LANES] = jnp.where(sel, p[:, sl], 0.0)
            run_eq = run_eq + jnp.sum(eq_t, axis=1, keepdims=True)
            run_sel = run_sel + jnp.sum(sel_t, axis=1, keepdims=True)

    block_first_slot.append(run_sel)
    lane = lax.broadcasted_iota(I32, (ne, LANES), 1)
    win0 = jnp.zeros((ne, LANES), I32)
    pages = jnp.zeros((ne, LANES), I32)
    for j in range(len(block_first_slot) - 1):
        first = block_first_slot[j].astype(I32)
        end = block_first_slot[j + 1].astype(I32)
        w0 = jnp.left_shift(jnp.right_shift(first, SLOT_ALIGN.bit_length() - 1), SLOT_ALIGN.bit_length() - 1)
        need_pages = jnp.right_shift(end - w0 + (SLOT_WIN - 1), SLOT_WIN.bit_length() - 1)
        need_pages = jnp.max(need_pages.astype(F32), axis=0, keepdims=True).astype(I32)
        win0 = jnp.where(lane == j, w0, win0)
        pages = jnp.where(lane == j, need_pages, pages)
    win_ref[...] = win0
    pages_ref[...] = pages


def _topk(probs_t, segs):
    b, ne, so = probs_t.shape
    assert so // TOK_BLOCK <= LANES
    blk = pl.BlockSpec((None, ne, so), lambda bi: (bi, 0, 0))
    tab = pl.BlockSpec((None, ne, LANES), lambda bi: (bi, 0, 0))
    pos, aff, win0, pages = pl.pallas_call(
        functools.partial(_topk_kernel, segs=segs),
        grid=(b,),
        in_specs=[blk],
        out_specs=[blk, blk, tab, tab],
        out_shape=[jax.ShapeDtypeStruct((b, ne, so), I32), jax.ShapeDtypeStruct((b, ne, so), F32),
                   jax.ShapeDtypeStruct((b, ne, LANES), I32), jax.ShapeDtypeStruct((b, ne, LANES), I32)],
        compiler_params=_cparams(("arbitrary",), VMEM_LIMIT),
        name="expert_choice_topk",
    )(probs_t)
    nblk = so // TOK_BLOCK
    win0 = jnp.swapaxes(win0[:, :, :nblk], 1, 2).reshape(-1)
    pages = pages[:, 0, :nblk].reshape(-1)
    return pos, aff, win0, pages


def _slot_windows(win_ref, base, page, n_slots):
    wins = []
    for e in range(N_EXPERTS):
        lo = win_ref[base + e] + page * SLOT_WIN
        wins.append((lo, pl.multiple_of(jnp.minimum(lo, n_slots - SLOT_WIN), SLOT_ALIGN)))
    return wins


def _window_hits(pos, wins):
    row = lax.broadcasted_iota(I32, (SLOT_WIN, TOK_BLOCK), 0)
    hits = []
    for e, (lo, start) in enumerate(wins):
        pe = pos[e:e + 1, :]
        hits.append(jnp.logical_and(pe - start == row, pe >= lo))
    return hits


def _moe_gather_kernel(win_ref, pages_ref, pos_ref, h_ref, xg_ref, *, n_slots):
    j = pl.program_id(1)
    blk = pl.program_id(0) * pl.num_programs(1) + j

    @pl.when(j == 0)
    def _():
        xg_ref[...] = jnp.zeros(xg_ref.shape, BF16)

    pos = pos_ref[...]

    def page(r, carry):
        wins = _slot_windows(win_ref, blk * N_EXPERTS, r, n_slots)
        onehot = jnp.concatenate([jnp.where(hit, 1.0, 0.0).astype(BF16) for hit in _window_hits(pos, wins)], axis=0)
        g = jnp.dot(onehot, h_ref[...], preferred_element_type=F32)
        for e, (_, start) in enumerate(wins):
            xg_ref[e, pl.ds(start, SLOT_WIN), :] += g[e * SLOT_WIN:(e + 1) * SLOT_WIN, :].astype(BF16)
        return carry

    lax.fori_loop(0, pages_ref[blk], page, 0)


def _moe_gather(win0, pages, pos, h2, n_slots):
    b, so, d = h2.shape
    ne = pos.shape[1]
    assert so % TOK_BLOCK == 0 and n_slots >= SLOT_WIN and n_slots % SLOT_ALIGN == 0 and ne == N_EXPERTS
    grid_spec = pltpu.PrefetchScalarGridSpec(
        num_scalar_prefetch=2,
        grid=(b, so // TOK_BLOCK),
        in_specs=[pl.BlockSpec((None, ne, TOK_BLOCK), lambda bi, j, w, p: (bi, 0, j)),
                  pl.BlockSpec((None, TOK_BLOCK, d), lambda bi, j, w, p: (bi, j, 0))],
        out_specs=pl.BlockSpec((None, ne, n_slots, d), lambda bi, j, w, p: (bi, 0, 0, 0)))
    return pl.pallas_call(
        functools.partial(_moe_gather_kernel, n_slots=n_slots),
        grid_spec=grid_spec,
        out_shape=jax.ShapeDtypeStruct((b, ne, n_slots, d), BF16),
        compiler_params=_cparams(("arbitrary", "arbitrary"), VMEM_LIMIT),
        name="moe_gather",
    )(win0, pages, pos, h2)


def _moe_ffn_kernel(x_ref, wg_ref, wu_ref, wd_ref, y_ref, wgb_ref, wub_ref, wdb_ref):
    @pl.when(pl.program_id(1) == 0)
    def _():
        wgb_ref[...] = wg_ref[...].astype(BF16)
        wub_ref[...] = wu_ref[...].astype(BF16)
        wdb_ref[...] = wd_ref[...].astype(BF16)

    xb = jnp.concatenate([x_ref[si] for si in range(x_ref.shape[0])], axis=0)
    gate = jnp.dot(xb, wgb_ref[...], preferred_element_type=F32)
    up = jnp.dot(xb, wub_ref[...], preferred_element_type=F32)
    hid = (_silu(gate) * up).astype(BF16)
    _put_rows(y_ref, jnp.dot(hid, wdb_ref[...], preferred_element_type=F32))


def _moe_ffn(xg, wg, wu, wd, layer):
    b, ne, n_slots, d = xg.shape
    ff = wg.shape[3]
    sb = _samples_per_step(b)
    rows = pl.BlockSpec((sb, None, n_slots, d), lambda e, bi: (bi, e, 0, 0))
    return pl.pallas_call(
        _moe_ffn_kernel,
        grid=(ne, b // sb),
        in_specs=[rows,
                  pl.BlockSpec((None, None, d, ff), lambda e, bi: (layer, e, 0, 0)),
                  pl.BlockSpec((None, None, d, ff), lambda e, bi: (layer, e, 0, 0)),
                  pl.BlockSpec((None, None, ff, d), lambda e, bi: (layer, e, 0, 0))],
        out_specs=rows,
        out_shape=jax.ShapeDtypeStruct((b, ne, n_slots, d), BF16),
        scratch_shapes=[pltpu.VMEM((d, ff), BF16), pltpu.VMEM((d, ff), BF16), pltpu.VMEM((ff, d), BF16)],
        compiler_params=_cparams(("arbitrary", "arbitrary"), VMEM_LIMIT),
        name="expert_ffn",
    )(xg, wg, wu, wd)


def _moe_combine_kernel(win_ref, pages_ref, pos_ref, aff_ref, y_ref, x_ref, mod_ref, o_ref, ycat_ref, acc_ref, *,
                        n_slots):
    blk = pl.program_id(0) * pl.num_programs(1) + pl.program_id(1)
    pos = pos_ref[...]
    aff = aff_ref[...]
    aff_hi = aff.astype(BF16).astype(F32)
    aff_lo = aff - aff_hi
    acc_ref[...] = jnp.zeros(acc_ref.shape, F32)
    tn = (((0,), (0,)), ((), ()))

    def page(r, carry):
        wins = _slot_windows(win_ref, blk * N_EXPERTS, r, n_slots)
        hits = _window_hits(pos, wins)
        for e, (_, start) in enumerate(wins):
            ycat_ref[e * SLOT_WIN:(e + 1) * SLOT_WIN, :] = y_ref[e, pl.ds(start, SLOT_WIN), :]
        w_hi = jnp.concatenate([jnp.where(hit, aff_hi[e:e + 1, :], 0.0).astype(BF16) for e, hit in enumerate(hits)],
                               axis=0)
        w_lo = jnp.concatenate([jnp.where(hit, aff_lo[e:e + 1, :], 0.0).astype(BF16) for e, hit in enumerate(hits)],
                               axis=0)
        ycat = ycat_ref[...]
        acc_ref[...] += (lax.dot_general(w_hi, ycat, tn, preferred_element_type=F32)
                         + lax.dot_general(w_lo, ycat, tn, preferred_element_type=F32))
        return carry

    lax.fori_loop(0, pages_ref[blk], page, 0)
    o_ref[...] = x_ref[...] + mod_ref[5:6, :] * acc_ref[...]


def _moe_combine(win0, pages, pos, aff, y, xmid, mod8, tile_off, n_ctx_tiles):
    b, so, d = xmid.shape
    _, ne, n_slots, _ = y.shape
    assert TOK_BLOCK == TOK_TILE
    if n_ctx_tiles and tile_off < n_ctx_tiles:
        seg = lambda i: jnp.minimum((i + tile_off) // n_ctx_tiles, 1)
    else:
        seg = lambda i: 1
    sel = pl.BlockSpec((None, ne, TOK_BLOCK), lambda bi, j, w, p: (bi, 0, j))
    tok = pl.BlockSpec((None, TOK_BLOCK, d), lambda bi, j, w, p: (bi, j, 0))
    grid_spec = pltpu.PrefetchScalarGridSpec(
        num_scalar_prefetch=2,
        grid=(b, so // TOK_BLOCK),
        in_specs=[sel, sel,
                  pl.BlockSpec((None, ne, n_slots, d), lambda bi, j, w, p: (bi, 0, 0, 0)),
                  tok,
                  pl.BlockSpec((None, None, 8, d), lambda bi, j, w, p: (bi, seg(j), 0, 0))],
        out_specs=tok,
        scratch_shapes=[pltpu.VMEM((ne * SLOT_WIN, d), BF16), pltpu.VMEM((TOK_BLOCK, d), F32)])
    return pl.pallas_call(
        functools.partial(_moe_combine_kernel, n_slots=n_slots),
        grid_spec=grid_spec,
        out_shape=jax.ShapeDtypeStruct((b, so, d), F32),
        compiler_params=_cparams(("arbitrary", "arbitrary"), VMEM_LIMIT),
        name="moe_combine",
    )(win0, pages, pos, aff, y, xmid, mod8)


def _rope_tables(ctx_len, lat_len):
    rows = lat_len // GRID_W
    row = jnp.repeat(jnp.arange(rows), GRID_W).astype(F32)
    col = jnp.tile(jnp.arange(GRID_W), rows).astype(F32)
    axis_dim = ATT_QK // 2
    inv = 1.0 / (ROPE_BASE ** (jnp.arange(0, axis_dim, 2, dtype=F32) / axis_dim))
    ang_row = row[:, None] * inv
    ang_col = col[:, None] * inv
    z = jnp.zeros_like(ang_row)
    cos64 = jnp.concatenate([jnp.cos(ang_row), jnp.cos(ang_row), jnp.cos(ang_col), jnp.cos(ang_col)], axis=-1)
    up64 = jnp.concatenate([-jnp.sin(ang_row), z, -jnp.sin(ang_col), z], axis=-1)
    dn64 = jnp.concatenate([z, jnp.sin(ang_row), z, jnp.sin(ang_col)], axis=-1)
    lat = jnp.stack([jnp.tile(t, (1, 2)) for t in (cos64, up64, dn64)])
    ctx = jnp.stack([jnp.ones((ctx_len, LANES), F32), jnp.zeros((ctx_len, LANES), F32),
                     jnp.zeros((ctx_len, LANES), F32)])
    return jnp.concatenate([ctx, lat], axis=1)


def kernel(x, c, ctx, c_ctx, mod_w, mod_b, norm1_g, norm2_g, w_in, w_out, att_q_norm_g, att_k_norm_g, att_lambda,
           att_subln_g, ret_log_decay, ret_norm_g, lru_conv_w, lru_conv_b, lru_gate_w, lru_gate_b, lru_lambda,
           lru_norm_g, router_w, exp_w_gate, exp_w_up, exp_w_down):
    b, lat_len, d = x.shape
    ctx_len = ctx.shape[1]
    depth = mod_w.shape[0]
    s = ctx_len + lat_len
    assert ctx_len % TOK_TILE == 0 and lat_len % KEY_BLOCK == 0 and lat_len % GRID_W == 0
    n_ctx_tiles = ctx_len // TOK_TILE

    rows = -(-(b + 1) // 8) * 8
    cc = jnp.zeros((rows, d), F32).at[:b].set(c).at[b].set(c_ctx)
    mod_all = _modulation(cc, mod_w, mod_b).reshape(depth, rows, 6, d)

    rope = _rope_tables(ctx_len, lat_len)
    lane = np.arange(2 * LANES)
    bd = jnp.asarray((lane[:, None] // 64) == (lane[None, :] // 64), BF16)
    ctx_src, lat_src = ctx, x

    out = None
    for i in range(depth):
        need_ctx = i < depth - 1
        m6 = mod_all[i]
        mod8 = jnp.stack([jnp.broadcast_to(m6[b], (b, 6, d)), m6[:b]], axis=1)
        mod8 = jnp.pad(mod8, ((0, 0), (0, 0), (0, 2), (0, 0)))
        gqk = jnp.stack([jnp.tile(att_q_norm_g[i], 2), jnp.tile(att_k_norm_g[i], 2)])
        q, kt, v, rq, rk, rv, gr, xu, gu = _inproj(ctx_src, lat_src, s, mod8, norm1_g[i][None],
                                                   w_in[i].astype(BF16), rope, gqk, bd, n_ctx_tiles)
        tile_off = 0 if need_ctx else n_ctx_tiles
        tok_off = tile_off * TOK_TILE
        lam_init = 0.8 - 0.6 * math.exp(-0.3 * i)
        lam_p = jnp.pad(att_lambda[i], ((0, 0), (0, LANES - ATT_QK)))
        att = _attention(lam_p, q, kt, v, att_subln_g[i][None], ctx_len, tile_off, n_ctx_tiles, lam_init)
        ret = _retention(ret_log_decay[i], rq, rk, rv, gr, jnp.tile(ret_norm_g[i], RET_HEADS)[None], bd, ctx_len,
                         tok_off)
        gw = lru_gate_w[i]
        gate_wd = jnp.zeros((2, 2, LRU_W, LRU_W), F32)
        bw = gw.shape[-1]
        for n in range(gw.shape[2]):
            gate_wd = gate_wd.at[:, :, n * bw:(n + 1) * bw, n * bw:(n + 1) * bw].set(gw[:, :, n])
        lru = _rglru(xu, gu, lru_conv_w[i], lru_conv_b[i][None], gate_wd.astype(BF16),
                     lru_gate_b[i].reshape(4, LRU_W), lru_lambda[i], lru_norm_g[i][None], ctx_len, tok_off)
        rw = jnp.pad(router_w[i], ((0, 0), (0, LANES - N_EXPERTS)))
        rw_hi = rw.astype(BF16)
        router2 = jnp.concatenate([rw_hi, (rw - rw_hi.astype(F32)).astype(BF16)], axis=1)
        xmid, h2, probs_t = _outproj(att, ret, lru, ctx_src, lat_src, mod8, norm2_g[i][None],
                                     w_out[i].astype(BF16), router2, tile_off, n_ctx_tiles)
        segs = []
        if need_ctx:
            segs.append((0, ctx_len, EC_CAPACITY * ctx_len // N_EXPERTS))
        segs.append((ctx_len - tok_off, lat_len, EC_CAPACITY * lat_len // N_EXPERTS))
        pos, aff, win0, pages = _topk(probs_t, tuple(segs))
        xg = _moe_gather(win0, pages, pos, h2, sum(k for _, _, k in segs))
        y = _moe_ffn(xg, exp_w_gate, exp_w_up, exp_w_down, i)
        out = _moe_combine(win0, pages, pos, aff, y, xmid, mod8, tile_off, n_ctx_tiles)
        ctx_src = lat_src = out
    return out
```

```python
import functools
import math

import jax
import jax.numpy as jnp
import numpy as np
from jax import lax
from jax.experimental import pallas as pl
from jax.experimental.pallas import tpu as pltpu

F32 = jnp.float32
BF16 = jnp.bfloat16
I32 = jnp.int32

ATT_HEADS = 4
ATT_QK = 64
ATT_V = 2 * ATT_QK
ATT_W = ATT_HEADS * ATT_V
RET_HEADS = 4
RET_QK = 64
RET_W = 256
LRU_W = 256
LRU_C = 8.0
N_EXPERTS = 16
EC_CAPACITY = 2
EPS = 1e-6
ROPE_BASE = 10000.0
GRID_W = 64
IN_COLS = 3072
Q_SCALE = ATT_QK ** -0.5 * math.log2(math.e)

LANES = 128
SUBLANES = 8
TOK_TILE = 256
KEY_BLOCK = 512
CHUNK = 128
RET_CHUNK = 256
TOK_BLOCK = 256
SLOT_WIN = 64
SLOT_ALIGN = 16
VMEM_LIMIT = 56 * 1024 * 1024


def _cparams(sem, vmem=None):
    return pltpu.CompilerParams(dimension_semantics=sem, vmem_limit_bytes=vmem)


def _sigmoid(x):
    return 0.5 * (jnp.tanh(0.5 * x) + 1.0)


def _silu(x):
    return x * _sigmoid(x)


def _group_mean_sq(t, bd):
    sq = t * t
    hi = sq.astype(BF16)
    lo = (sq - hi.astype(F32)).astype(BF16)
    ss = jnp.dot(hi, bd, preferred_element_type=F32) + jnp.dot(lo, bd, preferred_element_type=F32)
    return ss * (1.0 / 64.0)


def _mod_kernel(c_ref, w_ref, b_ref, o_ref):
    s = _silu(c_ref[...]).astype(BF16)
    o_ref[...] = jnp.dot(s, w_ref[...].astype(BF16), preferred_element_type=F32) + b_ref[...]


def _modulation(cc, mod_w, mod_b):
    depth, d, d6 = mod_w.shape
    rows = cc.shape[0]
    return pl.pallas_call(
        _mod_kernel,
        grid=(depth, d6 // d),
        in_specs=[pl.BlockSpec((rows, d), lambda l, j: (0, 0)),
                  pl.BlockSpec((None, d, d), lambda l, j: (l, 0, j)),
                  pl.BlockSpec((None, 1, d), lambda l, j: (l, 0, j))],
        out_specs=pl.BlockSpec((None, rows, d), lambda l, j: (l, 0, j)),
        out_shape=jax.ShapeDtypeStruct((depth, rows, d6), F32),
        compiler_params=_cparams(("arbitrary", "arbitrary")),
        name="modulation",
    )(cc, mod_w, mod_b.reshape(depth, 1, d6))


def _samples_per_step(b):
    return 2 if b % 2 == 0 else 1


def _token_sources(ctx_src, lat_src, tile_off, n_ctx_tiles, d, sb):
    lat_base = n_ctx_tiles if lat_src is ctx_src else 0
    ctx_spec = pl.BlockSpec((sb, TOK_TILE, d), lambda bi, i, *_: (bi, jnp.minimum(i + tile_off, n_ctx_tiles - 1), 0))
    lat_spec = pl.BlockSpec((sb, TOK_TILE, d),
                            lambda bi, i, *_: (bi, jnp.maximum(i + tile_off - n_ctx_tiles, 0) + lat_base, 0))
    return ctx_spec, lat_spec


def _put_rows(ref, val, cols=slice(None)):
    rows = ref.shape[1]
    for si in range(ref.shape[0]):
        ref[si, :, cols] = val[si * rows:(si + 1) * rows].astype(ref.dtype)


def _inproj_kernel(xc_ref, xl_ref, mod_ref, ng_ref, w_ref, rope_ref, gqk_ref, bd_ref,
                   q_ref, kt_ref, v_ref, rq_ref, rk_ref, rv_ref, gr_ref, xu_ref, gu_ref, *, n_ctx_tiles):
    sb, tm, _ = xc_ref.shape
    is_ctx = pl.program_id(1) < n_ctx_tiles
    hs = []
    for si in range(sb):
        x = jnp.where(is_ctx, xc_ref[si], xl_ref[si])
        ms = jnp.mean(x * x, axis=-1, keepdims=True)
        y = x * lax.rsqrt(ms + EPS) * ng_ref[...]
        hs.append((y * (1.0 + mod_ref[si, 1:2, :]) + mod_ref[si, 0:1, :]).astype(BF16))
    hb = jnp.concatenate(hs, axis=0)

    def proj(c0, c1):
        return jnp.dot(hb, w_ref[:, c0:c1], preferred_element_type=F32)

    bd = bd_ref[...]
    cosr = rope_ref[0]
    sin_up = rope_ref[1]
    sin_dn = rope_ref[2]

    def norm_rope(t, ms, g):
        tn = t * lax.rsqrt(ms + EPS) * g
        return tn * cosr + pltpu.roll(tn, LANES - 16, 1) * sin_up + pltpu.roll(tn, 16, 1) * sin_dn

    tqk = proj(0, 2 * ATT_W)
    for hd in range(ATT_HEADS):
        c0 = hd * ATT_V
        tq = tqk[:, c0:c0 + ATT_V]
        tk = tqk[:, ATT_W + c0:ATT_W + c0 + ATT_V]
        ms = _group_mean_sq(jnp.concatenate([tq, tk], axis=1), bd)
        for si in range(sb):
            rows = slice(si * tm, (si + 1) * tm)
            q_ref[si, :, c0:c0 + ATT_V] = (norm_rope(tq[rows], ms[rows, :ATT_V], gqk_ref[0:1, :])
                                           * Q_SCALE).astype(BF16)
            kt_ref[si, c0:c0 + ATT_V, :] = norm_rope(tk[rows], ms[rows, ATT_V:], gqk_ref[1:2, :]).T.astype(BF16)
    _put_rows(v_ref, proj(2 * ATT_W, 3 * ATT_W))
    base = 3 * ATT_W
    _put_rows(rq_ref, proj(base, base + RET_W))
    _put_rows(rk_ref, proj(base + RET_W, base + 2 * RET_W) * (RET_QK ** -0.5))
    _put_rows(rv_ref, proj(base + 2 * RET_W, base + 3 * RET_W))
    _put_rows(gr_ref, proj(base + 3 * RET_W, base + 4 * RET_W))
    _put_rows(xu_ref, proj(base + 4 * RET_W, base + 4 * RET_W + LRU_W))
    _put_rows(gu_ref, proj(base + 4 * RET_W + LRU_W, base + 4 * RET_W + 2 * LRU_W))


def _inproj(ctx_src, lat_src, s, mod8, norm_g, w_in_b, rope, gqk, bd, n_ctx_tiles):
    b, _, d = ctx_src.shape
    tm = TOK_TILE
    sb = _samples_per_step(b)
    seg = lambda i: jnp.minimum(i // n_ctx_tiles, 1)
    tok = lambda w: pl.BlockSpec((sb, tm, w), lambda bi, i: (bi, i, 0))
    f32o = lambda w: jax.ShapeDtypeStruct((b, s, w), F32)
    ctx_spec, lat_spec = _token_sources(ctx_src, lat_src, 0, n_ctx_tiles, d, sb)
    return pl.pallas_call(
        functools.partial(_inproj_kernel, n_ctx_tiles=n_ctx_tiles),
        grid=(b // sb, s // tm),
        in_specs=[ctx_spec, lat_spec,
                  pl.BlockSpec((sb, None, 8, d), lambda bi, i: (bi, seg(i), 0, 0)),
                  pl.BlockSpec((1, d), lambda bi, i: (0, 0)),
                  pl.BlockSpec((d, IN_COLS), lambda bi, i: (0, 0)),
                  pl.BlockSpec((3, tm, LANES), lambda bi, i: (0, i, 0)),
                  pl.BlockSpec((2, LANES), lambda bi, i: (0, 0)),
                  pl.BlockSpec((2 * LANES, 2 * LANES), lambda bi, i: (0, 0))],
        out_specs=[tok(ATT_W),
                   pl.BlockSpec((sb, ATT_W, tm), lambda bi, i: (bi, 0, i)),
                   tok(ATT_W), tok(RET_W), tok(RET_W), tok(RET_W), tok(RET_W), tok(LRU_W), tok(LRU_W)],
        out_shape=[jax.ShapeDtypeStruct((b, s, ATT_W), BF16),
                   jax.ShapeDtypeStruct((b, ATT_W, s), BF16),
                   jax.ShapeDtypeStruct((b, s, ATT_W), BF16),
                   f32o(RET_W), f32o(RET_W), f32o(RET_W), f32o(RET_W), f32o(LRU_W), f32o(LRU_W)],
        compiler_params=_cparams(("arbitrary", "arbitrary"), VMEM_LIMIT),
        name="inproj",
    )(ctx_src, lat_src, mod8, norm_g, w_in_b, rope, gqk, bd)


def _attn_kernel(lam_ref, q_ref, kt_ref, v_ref, g_ref, o_ref, *, ctx_len, q_off, n_ctx_tiles, lam_init):
    seq_len = kt_ref.shape[1]
    lp = lam_ref[...]
    lam = (jnp.exp(jnp.sum(lp[0:1, :] * lp[1:2, :], axis=1, keepdims=True))
           - jnp.exp(jnp.sum(lp[2:3, :] * lp[3:4, :], axis=1, keepdims=True)) + lam_init)
    q = q_ref[...]
    lane = lax.broadcasted_iota(I32, q.shape, 1)

    def attend(n_keys):
        outs = []
        for mi in range(2):
            keep = (lane < ATT_QK) if mi == 0 else (lane >= ATT_QK)
            qm = jnp.where(keep, q, jnp.zeros_like(q))
            s = jnp.dot(qm, kt_ref[:, 0:n_keys], preferred_element_type=F32)
            p = jnp.exp2(s - jnp.max(s, axis=1, keepdims=True))
            outs.append((p, jnp.sum(p, axis=1, keepdims=True)))
        a = outs[0][0] * (1.0 / outs[0][1]) - outs[1][0] * (lam / outs[1][1])
        o = jnp.dot(a.astype(BF16), v_ref[0:n_keys, :], preferred_element_type=F32)
        o = o * lax.rsqrt(jnp.mean(o * o, axis=-1, keepdims=True) + EPS) * g_ref[...] * (1.0 - lam_init)
        o_ref[...] = o.astype(BF16)

    if q_off >= n_ctx_tiles:
        attend(seq_len)
    else:
        is_ctx = pl.program_id(2) + q_off < n_ctx_tiles
        pl.when(is_ctx)(lambda: attend(ctx_len))
        pl.when(jnp.logical_not(is_ctx))(lambda: attend(seq_len))


def _attention(lam_p, q, kt, v, subln_g, ctx_len, q_off, n_ctx_tiles, lam_init):
    b, s, _ = q.shape
    tq = TOK_TILE
    nq = s // tq - q_off
    kern = functools.partial(_attn_kernel, ctx_len=ctx_len, q_off=q_off, n_ctx_tiles=n_ctx_tiles,
                             lam_init=lam_init)
    return pl.pallas_call(
        kern,
        grid=(b, ATT_HEADS, nq),
        in_specs=[pl.BlockSpec((4, LANES), lambda bi, h, i: (0, 0)),
                  pl.BlockSpec((None, tq, ATT_V), lambda bi, h, i: (bi, i + q_off, h)),
                  pl.BlockSpec((None, ATT_V, s), lambda bi, h, i: (bi, h, 0)),
                  pl.BlockSpec((None, s, ATT_V), lambda bi, h, i: (bi, 0, h)),
                  pl.BlockSpec((1, ATT_V), lambda bi, h, i: (0, 0))],
        out_specs=pl.BlockSpec((None, tq, ATT_V), lambda bi, h, i: (bi, i, h)),
        out_shape=jax.ShapeDtypeStruct((b, nq * tq, ATT_W), BF16),
        compiler_params=_cparams(("arbitrary", "arbitrary", "arbitrary"), VMEM_LIMIT),
        name="diff_attention",
    )(lam_p, q, kt, v, subln_g)


def _ret_kernel(lg_ref, q_ref, k_ref, v_ref, gr_ref, g_ref, bd_ref, o_ref,
                of_ref, ob_ref, st_ref, dm_ref, cr_ref, in_ref, cd_ref, *, n_ctx_chunks, n_chunks, out_chunk0):
    c = RET_CHUNK
    rown = lax.broadcasted_iota(I32, (c, c), 0)
    colm = lax.broadcasted_iota(I32, (c, c), 1)
    low = lax.broadcasted_iota(I32, (c, LANES), 1) < RET_QK
    rowf = lax.broadcasted_iota(I32, (c, LANES), 0).astype(F32)
    srow = lax.broadcasted_iota(I32, (LANES, LANES), 0)
    scol = lax.broadcasted_iota(I32, (LANES, LANES), 1)
    for d in range(2):
        if d == 0:
            diff = (rown - colm).astype(F32)
            mask = rown >= colm
            cross_pw = rowf + 1.0
            inner_pw = (c - 1.0) - rowf
        else:
            diff = (colm - rown).astype(F32)
            mask = colm > rown
            cross_pw = c - rowf
            inner_pw = rowf
        for hp in range(2):
            dm_ref[d, hp] = jnp.concatenate(
                [jnp.where(mask, jnp.exp(lg_ref[d, 2 * hp + j] * jnp.where(mask, diff, 0.0)), 0.0) for j in range(2)],
                axis=1)
            lg_lane = jnp.where(low, lg_ref[d, 2 * hp], lg_ref[d, 2 * hp + 1])
            cr_ref[d, hp] = jnp.exp(lg_lane * cross_pw)
            in_ref[d, hp] = jnp.exp(lg_lane * inner_pw)
            cd_ref[d, hp] = jnp.exp(jnp.where(srow < RET_QK, lg_ref[d, 2 * hp], lg_ref[d, 2 * hp + 1]) * float(c))
    st_ref[...] = jnp.zeros(st_ref.shape, F32)
    bdmask = (srow < RET_QK) == (scol < RET_QK)

    def step(i, carry):
        for d in range(2):
            if d == 0:
                ci = i
            else:
                ci = jnp.where(i < n_ctx_chunks, n_ctx_chunks - 1 - i, n_chunks - 1 - (i - n_ctx_chunks))
            r0 = pl.multiple_of(ci * c, c)
            dst = of_ref if d == 0 else ob_ref
            for hp in range(2):
                cols = slice(hp * LANES, (hp + 1) * LANES)
                kf = k_ref[pl.ds(r0, c), cols]
                qf = q_ref[pl.ds(r0, c), cols]
                kb = kf.astype(BF16)
                vb = v_ref[pl.ds(r0, c), cols].astype(BF16)
                zero = jnp.zeros_like(kb)
                nt = (((1,), (1,)), ((), ()))
                kcat = jnp.concatenate([jnp.where(low, kb, zero), jnp.where(low, zero, kb)], axis=0)
                sc = (lax.dot_general(qf.astype(BF16), kcat, nt, preferred_element_type=F32)
                      * dm_ref[d, hp]).astype(BF16)
                vbd = jnp.concatenate([jnp.where(low, vb, zero), jnp.where(low, zero, vb)], axis=0)
                st = st_ref[d, hp]
                intra = jnp.dot(sc, vbd, preferred_element_type=F32)
                inter = jnp.dot(qf.astype(BF16), st.astype(BF16), preferred_element_type=F32) * cr_ref[d, hp]
                dst[pl.ds(r0, c), cols] = intra + inter
                kin = (kf * in_ref[d, hp]).astype(BF16)
                kv = lax.dot_general(kin, vb, (((0,), (0,)), ((), ())), preferred_element_type=F32)
                st_ref[d, hp] = st * cd_ref[d, hp] + jnp.where(bdmask, kv, 0.0)
        return carry

    lax.fori_loop(0, n_chunks, step, 0, unroll=2)

    bd = bd_ref[...]

    def finish(i, carry):
        r0 = pl.multiple_of((i + out_chunk0) * c, c)
        ro = pl.multiple_of(i * c, c)
        o = of_ref[pl.ds(r0, c), :] + ob_ref[pl.ds(r0, c), :]
        on = o * lax.rsqrt(_group_mean_sq(o, bd) + EPS) * g_ref[...]
        o_ref[pl.ds(ro, c), :] = (on * _silu(gr_ref[pl.ds(r0, c), :])).astype(BF16)
        return carry

    lax.fori_loop(0, n_chunks - out_chunk0, finish, 0)


def _retention(log_decay, rq, rk, rv, gr, norm_g, bd, ctx_len, out_off):
    b, s, w = rq.shape
    c = RET_CHUNK
    assert ctx_len % c == 0 and s % c == 0 and out_off % c == 0 and w == 2 * LANES
    n_chunks = s // c
    out_chunk0 = out_off // c
    kern = functools.partial(_ret_kernel, n_ctx_chunks=ctx_len // c, n_chunks=n_chunks, out_chunk0=out_chunk0)
    seq = pl.BlockSpec((None, s, w), lambda bi: (bi, 0, 0))
    state = pltpu.VMEM((2, 2, LANES, LANES), F32)
    rows = pltpu.VMEM((2, 2, c, LANES), F32)
    return pl.pallas_call(
        kern,
        grid=(b,),
        in_specs=[pl.BlockSpec(memory_space=pltpu.SMEM), seq, seq, seq, seq,
                  pl.BlockSpec((1, w), lambda bi: (0, 0)),
                  pl.BlockSpec((2 * LANES, 2 * LANES), lambda bi: (0, 0))],
        out_specs=pl.BlockSpec((None, s - out_off, w), lambda bi: (bi, 0, 0)),
        out_shape=jax.ShapeDtypeStruct((b, s - out_off, w), BF16),
        scratch_shapes=[pltpu.VMEM((s, w), F32), pltpu.VMEM((s, w), F32), state,
                        pltpu.VMEM((2, 2, c, 2 * c), F32), rows, rows, state],
        compiler_params=_cparams(("arbitrary",), VMEM_LIMIT),
        name="retention",
    )(log_decay, rq, rk, rv, gr, norm_g, bd)


def _gelu_tanh(x):
    return 0.5 * x * (1.0 + jnp.tanh(math.sqrt(2.0 / math.pi) * (x + 0.044715 * (x * x * x))))


def _lru_kernel(xu_ref, gu_ref, cw_ref, cb_ref, gw_ref, gb_ref, lam_ref, ng_ref, o_ref,
                u_ref, hf_ref, hb_ref, *, ctx_len, seq_len, out_chunk0):
    c = CHUNK
    n_chunks = seq_len // c
    n_ctx_chunks = ctx_len // c
    row = lax.broadcasted_iota(I32, (c, LRU_W), 0)

    def conv(i, carry):
        r0 = pl.multiple_of(i * c, c)
        prev0 = pl.multiple_of(jnp.maximum(r0 - 8, 0), 8)
        next0 = pl.multiple_of(jnp.minimum(r0 + c, seq_len - 8), 8)
        ext = jnp.concatenate([xu_ref[pl.ds(prev0, 8), :], xu_ref[pl.ds(r0, c), :], xu_ref[pl.ds(next0, 8), :]],
                              axis=0)
        seg_first = jnp.logical_or(i == 0, i == n_ctx_chunks)
        seg_last = jnp.logical_or(i == n_ctx_chunks - 1, i == n_chunks - 1)
        xm2 = pltpu.roll(ext, 2, 0)[8:8 + c]
        xm1 = pltpu.roll(ext, 1, 0)[8:8 + c]
        xp1 = pltpu.roll(ext, c + 16 - 1, 0)[8:8 + c]
        xm2 = jnp.where(jnp.logical_and(seg_first, row < 2), 0.0, xm2)
        xm1 = jnp.where(jnp.logical_and(seg_first, row < 1), 0.0, xm1)
        xp1 = jnp.where(jnp.logical_and(seg_last, row >= c - 1), 0.0, xp1)
        u_ref[pl.ds(r0, c), :] = (cw_ref[0:1, :] * xm2 + cw_ref[1:2, :] * xm1 + cw_ref[2:3, :] * ext[8:8 + c]
                                  + cw_ref[3:4, :] * xp1 + cb_ref[...])
        return carry

    lax.fori_loop(0, n_chunks, conv, 0)

    def softplus(z):
        return jnp.maximum(z, 0.0) + jnp.log1p(jnp.exp(-jnp.abs(z)))

    def scan_dir(d, u):
        ub = u.astype(BF16)
        zr = jnp.dot(ub, gw_ref[d, 0], preferred_element_type=F32) + gb_ref[2 * d:2 * d + 1, :]
        zi = jnp.dot(ub, gw_ref[d, 1], preferred_element_type=F32) + gb_ref[2 * d + 1:2 * d + 2, :]
        log_a = -LRU_C * softplus(-lam_ref[d:d + 1, :]) * _sigmoid(zr)
        a = jnp.exp(log_a)
        b = jnp.sqrt(-jnp.tanh(log_a) * (a * a + 1.0)) * (_sigmoid(zi) * u)
        ng = c // SUBLANES
        a = a.reshape(ng, SUBLANES, LRU_W)
        b = b.reshape(ng, SUBLANES, LRU_W)
        sub = lax.broadcasted_iota(I32, (ng, SUBLANES, LRU_W), 1)
        sh = 1
        while sh < SUBLANES:
            if d == 0:
                valid = sub >= sh
                a_prev = pltpu.roll(a, sh, 1)
                b_prev = pltpu.roll(b, sh, 1)
            else:
                valid = sub < SUBLANES - sh
                a_prev = pltpu.roll(a, SUBLANES - sh, 1)
                b_prev = pltpu.roll(b, SUBLANES - sh, 1)
            b = a * jnp.where(valid, b_prev, 0.0) + b
            a = a * jnp.where(valid, a_prev, 1.0)
            sh *= 2
        return a, b

    def chain(d, a, b, h):
        ng = c // SUBLANES
        out = [None] * ng
        for g in (range(ng) if d == 0 else range(ng - 1, -1, -1)):
            hg = a[g] * h + b[g]
            out[g] = hg
            h = hg[SUBLANES - 1:SUBLANES, :] if d == 0 else hg[0:1, :]
        return jnp.concatenate(out, axis=0), h

    def step(i, carry):
        h_f, h_b = carry
        r0 = pl.multiple_of(i * c, c)
        a, b = scan_dir(0, u_ref[pl.ds(r0, c), :])
        hh, h_f = chain(0, a, b, h_f)
        hf_ref[pl.ds(r0, c), :] = hh
        cb = jnp.where(i < n_ctx_chunks, n_ctx_chunks - 1 - i, n_chunks - 1 - (i - n_ctx_chunks))
        rb = pl.multiple_of(cb * c, c)
        a, b = scan_dir(1, u_ref[pl.ds(rb, c), :])
        hh, h_b = chain(1, a, b, h_b)
        hb_ref[pl.ds(rb, c), :] = hh
        return h_f, h_b

    zero = jnp.zeros((1, LRU_W), F32)
    lax.fori_loop(0, n_chunks, step, (zero, zero))

    def finish(i, carry):
        r0 = pl.multiple_of((i + out_chunk0) * c, c)
        ro = pl.multiple_of(i * c, c)
        t = (hf_ref[pl.ds(r0, c), :] + hb_ref[pl.ds(r0, c), :]) * _gelu_tanh(gu_ref[pl.ds(r0, c), :])
        t = t * lax.rsqrt(jnp.mean(t * t, axis=-1, keepdims=True) + EPS) * ng_ref[...]
        o_ref[pl.ds(ro, c), :] = t.astype(BF16)
        return carry

    lax.fori_loop(0, n_chunks - out_chunk0, finish, 0)


def _rglru(xu, gu, conv_w, conv_b, gate_wd, gate_b, lam, norm_g, ctx_len, out_off):
    b, s, w = xu.shape
    kern = functools.partial(_lru_kernel, ctx_len=ctx_len, seq_len=s, out_chunk0=out_off // CHUNK)
    seq = pl.BlockSpec((None, s, w), lambda bi: (bi, 0, 0))
    full = lambda shp: pl.BlockSpec(shp, lambda bi: (0,) * len(shp))
    return pl.pallas_call(
        kern,
        grid=(b,),
        in_specs=[seq, seq, full((4, w)), full((1, w)), full((2, 2, w, w)), full((4, w)), full((2, w)),
                  full((1, w))],
        out_specs=pl.BlockSpec((None, s - out_off, w), lambda bi: (bi, 0, 0)),
        out_shape=jax.ShapeDtypeStruct((b, s - out_off, w), BF16),
        scratch_shapes=[pltpu.VMEM((s, w), F32), pltpu.VMEM((s, w), F32), pltpu.VMEM((s, w), F32)],
        compiler_params=_cparams(("arbitrary",), VMEM_LIMIT),
        name="rglru",
    )(xu, gu, conv_w, conv_b, gate_wd, gate_b, lam, norm_g)


def _outproj_kernel(att_ref, ret_ref, lru_ref, xc_ref, xl_ref, mod_ref, ng_ref, w_ref, rw_ref,
                    xmid_ref, h2_ref, pt_ref, *, tile_off, n_ctx_tiles):
    sb, tm, _ = att_ref.shape
    mix = jnp.concatenate([jnp.concatenate([att_ref[si], ret_ref[si], lru_ref[si]], axis=1) for si in range(sb)],
                          axis=0)
    y = jnp.dot(mix, w_ref[...], preferred_element_type=F32)
    is_ctx = pl.program_id(1) + tile_off < n_ctx_tiles
    hs = []
    for si in range(sb):
        x = jnp.where(is_ctx, xc_ref[si], xl_ref[si]) + mod_ref[si, 2:3, :] * y[si * tm:(si + 1) * tm]
        xmid_ref[si] = x
        ms = jnp.mean(x * x, axis=-1, keepdims=True)
        hs.append((x * lax.rsqrt(ms + EPS) * ng_ref[...]) * (1.0 + mod_ref[si, 4:5, :]) + mod_ref[si, 3:4, :])
    h = jnp.concatenate(hs, axis=0)
    hb = h.astype(BF16)
    _put_rows(h2_ref, hb)
    hlo = (h - hb.astype(F32)).astype(BF16)
    wide = jnp.dot(hb, rw_ref[...], preferred_element_type=F32)
    logits = (wide[:, :LANES] + wide[:, LANES:]
              + jnp.dot(hlo, rw_ref[:, :LANES], preferred_element_type=F32))
    lane = lax.broadcasted_iota(I32, logits.shape, 1)
    logits = jnp.where(lane < N_EXPERTS, logits, -jnp.inf)
    e = jnp.exp(logits - jnp.max(logits, axis=-1, keepdims=True))
    probs = e / jnp.sum(e, axis=-1, keepdims=True)
    for si in range(sb):
        pt_ref[si] = probs[si * tm:(si + 1) * tm].T[0:N_EXPERTS, :]


def _outproj(att, ret, lru, ctx_src, lat_src, mod8, norm_g, w_out_b, router2, tile_off, n_ctx_tiles):
    b, so, _ = att.shape
    d = ctx_src.shape[-1]
    tm = TOK_TILE
    sb = 1
    seg = lambda i: jnp.minimum((i + tile_off) // n_ctx_tiles, 1)
    tok = lambda w: pl.BlockSpec((sb, tm, w), lambda bi, i: (bi, i, 0))
    ctx_spec, lat_spec = _token_sources(ctx_src, lat_src, tile_off, n_ctx_tiles, d, sb)
    return pl.pallas_call(
        functools.partial(_outproj_kernel, tile_off=tile_off, n_ctx_tiles=n_ctx_tiles),
        grid=(b // sb, so // tm),
        in_specs=[tok(ATT_W), tok(RET_W), tok(LRU_W), ctx_spec, lat_spec,
                  pl.BlockSpec((sb, None, 8, d), lambda bi, i: (bi, seg(i), 0, 0)),
                  pl.BlockSpec((1, d), lambda bi, i: (0, 0)),
                  pl.BlockSpec((d, d), lambda bi, i: (0, 0)),
                  pl.BlockSpec((d, 2 * LANES), lambda bi, i: (0, 0))],
        out_specs=[tok(d), tok(d), pl.BlockSpec((sb, N_EXPERTS, tm), lambda bi, i: (bi, 0, i))],
        out_shape=[jax.ShapeDtypeStruct((b, so, d), F32), jax.ShapeDtypeStruct((b, so, d), BF16),
                   jax.ShapeDtypeStruct((b, N_EXPERTS, so), F32)],
        compiler_params=_cparams(("arbitrary", "arbitrary"), VMEM_LIMIT),
        name="outproj_router",
    )(att, ret, lru, ctx_src, lat_src, mod8, norm_g, w_out_b, router2)


def _topk_kernel(p_ref, pos_ref, aff_ref, win_ref, pages_ref, *, segs):
    ne = p_ref.shape[0]
    block_first_slot = []
    ri = lax.broadcasted_iota(I32, (LANES, LANES), 0)
    ci = lax.broadcasted_iota(I32, (LANES, LANES), 1)
    before = jnp.where(ri < ci, 1.0, 0.0).astype(BF16)
    run_sel = jnp.zeros((ne, 1), F32)

    def kth_floor(vals, k):
        t = jnp.zeros((ne, 1), I32)
        for bit in range(30, -1, -1):
            cand = t | (1 << bit)
            cnt = jnp.sum(jnp.where(vals >= pltpu.bitcast(cand, F32), 1.0, 0.0), axis=1, keepdims=True)
            t = jnp.where(cnt >= k, cand, t)
        return pltpu.bitcast(t, F32)

    for start, ln, k in segs:
        p = p_ref[:, start:start + ln]
        resid = p - kth_floor(p, k)
        t = kth_floor(resid, k)
        gt = resid > t
        eq = resid == t
        need = k - jnp.sum(jnp.where(gt, 1.0, 0.0), axis=1, keepdims=True)
        run_eq = jnp.zeros((ne, 1), F32)
        for j in range(ln // LANES):
            sl = slice(j * LANES, (j + 1) * LANES)
            if (start + j * LANES) % TOK_BLOCK == 0:
                block_first_slot.append(run_sel)
            eq_t = jnp.where(eq[:, sl], 1.0, 0.0)
            rank_eq = jnp.dot(eq_t.astype(BF16), before, preferred_element_type=F32) + run_eq
            sel = jnp.logical_or(gt[:, sl], jnp.logical_and(eq[:, sl], rank_eq < need))
            sel_t = jnp.where(sel, 1.0, 0.0)
            slot = jnp.dot(sel_t.astype(BF16), before, preferred_element_type=F32) + run_sel
            pos_ref[:, start + j * LANES:start + (j + 1) * LANES] = jnp.where(sel, slot, -1.0).astype(I32)
            aff_ref[:, start + j * LANES:start + (j + 1) * LANES] = jnp.where(sel, p[:, sl], 0.0)
            run_eq = run_eq + jnp.sum(eq_t, axis=1, keepdims=True)
            run_sel = run_sel + jnp.sum(sel_t, axis=1, keepdims=True)

    block_first_slot.append(run_sel)
    lane = lax.broadcasted_iota(I32, (ne, LANES), 1)
    win0 = jnp.zeros((ne, LANES), I32)
    pages = jnp.zeros((ne, LANES), I32)
    for j in range(len(block_first_slot) - 1):
        first = block_first_slot[j].astype(I32)
        end = block_first_slot[j + 1].astype(I32)
        w0 = jnp.left_shift(jnp.right_shift(first, SLOT_ALIGN.bit_length() - 1), SLOT_ALIGN.bit_length() - 1)
        need_pages = jnp.right_shift(end - w0 + (SLOT_WIN - 1), SLOT_WIN.bit_length() - 1)
        per_sample = jnp.max(need_pages.astype(F32).reshape(ne // N_EXPERTS, N_EXPERTS, 1), axis=1, keepdims=True)
        need_pages = jnp.broadcast_to(per_sample, (ne // N_EXPERTS, N_EXPERTS, 1)).reshape(ne, 1).astype(I32)
        win0 = jnp.where(lane == j, w0, win0)
        pages = jnp.where(lane == j, need_pages, pages)
    win_ref[...] = win0
    pages_ref[...] = pages


def _topk(probs_t, segs):
    b, ne, so = probs_t.shape
    assert so // TOK_BLOCK <= LANES and ne == N_EXPERTS
    rows = b * ne
    blk = pl.BlockSpec((rows, so), lambda i: (0, 0))
    tab = pl.BlockSpec((rows, LANES), lambda i: (0, 0))
    pos, aff, win0, pages = pl.pallas_call(
        functools.partial(_topk_kernel, segs=segs),
        grid=(1,),
        in_specs=[blk],
        out_specs=[blk, blk, tab, tab],
        out_shape=[jax.ShapeDtypeStruct((rows, so), I32), jax.ShapeDtypeStruct((rows, so), F32),
                   jax.ShapeDtypeStruct((rows, LANES), I32), jax.ShapeDtypeStruct((rows, LANES), I32)],
        compiler_params=_cparams(("arbitrary",), VMEM_LIMIT),
        name="expert_choice_topk",
    )(probs_t.reshape(rows, so))
    pos, aff = pos.reshape(b, ne, so), aff.reshape(b, ne, so)
    win0, pages = win0.reshape(b, ne, LANES), pages.reshape(b, ne, LANES)
    nblk = so // TOK_BLOCK
    win0 = jnp.swapaxes(win0[:, :, :nblk], 1, 2).reshape(-1)
    pages = pages[:, 0, :nblk].reshape(-1)
    return pos, aff, win0, pages


def _slot_windows(win_ref, base, page, n_slots):
    wins = []
    for e in range(N_EXPERTS):
        lo = win_ref[base + e] + page * SLOT_WIN
        wins.append((lo, pl.multiple_of(jnp.minimum(lo, n_slots - SLOT_WIN), SLOT_ALIGN)))
    return wins


def _window_hits(pos, wins):
    row = lax.broadcasted_iota(I32, (SLOT_WIN, TOK_BLOCK), 0)
    hits = []
    for e, (lo, start) in enumerate(wins):
        pe = pos[e:e + 1, :]
        hits.append(jnp.logical_and(pe - start == row, pe >= lo))
    return hits


def _moe_gather_kernel(win_ref, pages_ref, pos_ref, h_ref, xg_ref, *, n_slots):
    j = pl.program_id(1)
    blk = pl.program_id(0) * pl.num_programs(1) + j

    @pl.when(j == 0)
    def _():
        xg_ref[...] = jnp.zeros(xg_ref.shape, BF16)

    pos = pos_ref[...]

    def page(r, carry):
        wins = _slot_windows(win_ref, blk * N_EXPERTS, r, n_slots)
        onehot = jnp.concatenate([jnp.where(hit, 1.0, 0.0).astype(BF16) for hit in _window_hits(pos, wins)], axis=0)
        g = jnp.dot(onehot, h_ref[...], preferred_element_type=F32)
        for e, (_, start) in enumerate(wins):
            xg_ref[e, pl.ds(start, SLOT_WIN), :] += g[e * SLOT_WIN:(e + 1) * SLOT_WIN, :].astype(BF16)
        return carry

    lax.fori_loop(0, pages_ref[blk], page, 0)


def _moe_gather(win0, pages, pos, h2, n_slots):
    b, so, d = h2.shape
    ne = pos.shape[1]
    assert so % TOK_BLOCK == 0 and n_slots >= SLOT_WIN and n_slots % SLOT_ALIGN == 0 and ne == N_EXPERTS
    grid_spec = pltpu.PrefetchScalarGridSpec(
        num_scalar_prefetch=2,
        grid=(b, so // TOK_BLOCK),
        in_specs=[pl.BlockSpec((None, ne, TOK_BLOCK), lambda bi, j, w, p: (bi, 0, j)),
                  pl.BlockSpec((None, TOK_BLOCK, d), lambda bi, j, w, p: (bi, j, 0))],
        out_specs=pl.BlockSpec((None, ne, n_slots, d), lambda bi, j, w, p: (bi, 0, 0, 0)))
    return pl.pallas_call(
        functools.partial(_moe_gather_kernel, n_slots=n_slots),
        grid_spec=grid_spec,
        out_shape=jax.ShapeDtypeStruct((b, ne, n_slots, d), BF16),
        compiler_params=_cparams(("arbitrary", "arbitrary"), VMEM_LIMIT),
        name="moe_gather",
    )(win0, pages, pos, h2)


def _moe_ffn_kernel(x_ref, wg_ref, wu_ref, wd_ref, y_ref, wgb_ref, wub_ref, wdb_ref):
    @pl.when(pl.program_id(1) == 0)
    def _():
        wgb_ref[...] = wg_ref[...].astype(BF16)
        wub_ref[...] = wu_ref[...].astype(BF16)
        wdb_ref[...] = wd_ref[...].astype(BF16)

    xb = jnp.concatenate([x_ref[si] for si in range(x_ref.shape[0])], axis=0)
    gate = jnp.dot(xb, wgb_ref[...], preferred_element_type=F32)
    up = jnp.dot(xb, wub_ref[...], preferred_element_type=F32)
    hid = (_silu(gate) * up).astype(BF16)
    _put_rows(y_ref, jnp.dot(hid, wdb_ref[...], preferred_element_type=F32))


def _moe_ffn(xg, wg, wu, wd, layer):
    b, ne, n_slots, d = xg.shape
    ff = wg.shape[3]
    sb = _samples_per_step(b)
    rows = pl.BlockSpec((sb, None, n_slots, d), lambda e, bi: (bi, e, 0, 0))
    return pl.pallas_call(
        _moe_ffn_kernel,
        grid=(ne, b // sb),
        in_specs=[rows,
                  pl.BlockSpec((None, None, d, ff), lambda e, bi: (layer, e, 0, 0)),
                  pl.BlockSpec((None, None, d, ff), lambda e, bi: (layer, e, 0, 0)),
                  pl.BlockSpec((None, None, ff, d), lambda e, bi: (layer, e, 0, 0))],
        out_specs=rows,
        out_shape=jax.ShapeDtypeStruct((b, ne, n_slots, d), BF16),
        scratch_shapes=[pltpu.VMEM((d, ff), BF16), pltpu.VMEM((d, ff), BF16), pltpu.VMEM((ff, d), BF16)],
        compiler_params=_cparams(("arbitrary", "arbitrary"), VMEM_LIMIT),
        name="expert_ffn",
    )(xg, wg, wu, wd)


def _moe_combine_kernel(win_ref, pages_ref, pos_ref, aff_ref, y_ref, x_ref, mod_ref, o_ref, ycat_ref, acc_ref, *,
                        n_slots):
    blk = pl.program_id(0) * pl.num_programs(1) + pl.program_id(1)
    pos = pos_ref[...]
    aff = aff_ref[...]
    aff_hi = aff.astype(BF16).astype(F32)
    aff_lo = aff - aff_hi
    acc_ref[...] = jnp.zeros(acc_ref.shape, F32)
    tn = (((0,), (0,)), ((), ()))

    def page(r, carry):
        wins = _slot_windows(win_ref, blk * N_EXPERTS, r, n_slots)
        hits = _window_hits(pos, wins)
        for e, (_, start) in enumerate(wins):
            ycat_ref[e * SLOT_WIN:(e + 1) * SLOT_WIN, :] = y_ref[e, pl.ds(start, SLOT_WIN), :]
        w_hi = jnp.concatenate([jnp.where(hit, aff_hi[e:e + 1, :], 0.0).astype(BF16) for e, hit in enumerate(hits)],
                               axis=0)
        w_lo = jnp.concatenate([jnp.where(hit, aff_lo[e:e + 1, :], 0.0).astype(BF16) for e, hit in enumerate(hits)],
                               axis=0)
        ycat = ycat_ref[...]
        acc_ref[...] += (lax.dot_general(w_hi, ycat, tn, preferred_element_type=F32)
                         + lax.dot_general(w_lo, ycat, tn, preferred_element_type=F32))
        return carry

    lax.fori_loop(0, pages_ref[blk], page, 0)
    o_ref[...] = x_ref[...] + mod_ref[5:6, :] * acc_ref[...]


def _moe_combine(win0, pages, pos, aff, y, xmid, mod8, tile_off, n_ctx_tiles):
    b, so, d = xmid.shape
    _, ne, n_slots, _ = y.shape
    assert TOK_BLOCK == TOK_TILE
    if n_ctx_tiles and tile_off < n_ctx_tiles:
        seg = lambda i: jnp.minimum((i + tile_off) // n_ctx_tiles, 1)
    else:
        seg = lambda i: 1
    sel = pl.BlockSpec((None, ne, TOK_BLOCK), lambda bi, j, w, p: (bi, 0, j))
    tok = pl.BlockSpec((None, TOK_BLOCK, d), lambda bi, j, w, p: (bi, j, 0))
    grid_spec = pltpu.PrefetchScalarGridSpec(
        num_scalar_prefetch=2,
        grid=(b, so // TOK_BLOCK),
        in_specs=[sel, sel,
                  pl.BlockSpec((None, ne, n_slots, d), lambda bi, j, w, p: (bi, 0, 0, 0)),
                  tok,
                  pl.BlockSpec((None, None, 8, d), lambda bi, j, w, p: (bi, seg(j), 0, 0))],
        out_specs=tok,
        scratch_shapes=[pltpu.VMEM((ne * SLOT_WIN, d), BF16), pltpu.VMEM((TOK_BLOCK, d), F32)])
    return pl.pallas_call(
        functools.partial(_moe_combine_kernel, n_slots=n_slots),
        grid_spec=grid_spec,
        out_shape=jax.ShapeDtypeStruct((b, so, d), F32),
        compiler_params=_cparams(("arbitrary", "arbitrary"), VMEM_LIMIT),
        name="moe_combine",
    )(win0, pages, pos, aff, y, xmid, mod8)


def _rope_tables(ctx_len, lat_len):
    rows = lat_len // GRID_W
    row = jnp.repeat(jnp.arange(rows), GRID_W).astype(F32)
    col = jnp.tile(jnp.arange(GRID_W), rows).astype(F32)
    axis_dim = ATT_QK // 2
    inv = 1.0 / (ROPE_BASE ** (jnp.arange(0, axis_dim, 2, dtype=F32) / axis_dim))
    ang_row = row[:, None] * inv
    ang_col = col[:, None] * inv
    z = jnp.zeros_like(ang_row)
    cos64 = jnp.concatenate([jnp.cos(ang_row), jnp.cos(ang_row), jnp.cos(ang_col), jnp.cos(ang_col)], axis=-1)
    up64 = jnp.concatenate([-jnp.sin(ang_row), z, -jnp.sin(ang_col), z], axis=-1)
    dn64 = jnp.concatenate([z, jnp.sin(ang_row), z, jnp.sin(ang_col)], axis=-1)
    lat = jnp.stack([jnp.tile(t, (1, 2)) for t in (cos64, up64, dn64)])
    ctx = jnp.stack([jnp.ones((ctx_len, LANES), F32), jnp.zeros((ctx_len, LANES), F32),
                     jnp.zeros((ctx_len, LANES), F32)])
    return jnp.concatenate([ctx, lat], axis=1)


def kernel(x, c, ctx, c_ctx, mod_w, mod_b, norm1_g, norm2_g, w_in, w_out, att_q_norm_g, att_k_norm_g, att_lambda,
           att_subln_g, ret_log_decay, ret_norm_g, lru_conv_w, lru_conv_b, lru_gate_w, lru_gate_b, lru_lambda,
           lru_norm_g, router_w, exp_w_gate, exp_w_up, exp_w_down):
    b, lat_len, d = x.shape
    ctx_len = ctx.shape[1]
    depth = mod_w.shape[0]
    s = ctx_len + lat_len
    assert ctx_len % TOK_TILE == 0 and lat_len % KEY_BLOCK == 0 and lat_len % GRID_W == 0
    n_ctx_tiles = ctx_len // TOK_TILE

    rows = -(-(b + 1) // 8) * 8
    cc = jnp.zeros((rows, d), F32).at[:b].set(c).at[b].set(c_ctx)
    mod_all = _modulation(cc, mod_w, mod_b).reshape(depth, rows, 6, d)

    rope = _rope_tables(ctx_len, lat_len)
    lane = np.arange(2 * LANES)
    bd = jnp.asarray((lane[:, None] // 64) == (lane[None, :] // 64), BF16)
    ctx_src, lat_src = ctx, x

    out = None
    for i in range(depth):
        need_ctx = i < depth - 1
        m6 = mod_all[i]
        mod8 = jnp.stack([jnp.broadcast_to(m6[b], (b, 6, d)), m6[:b]], axis=1)
        mod8 = jnp.pad(mod8, ((0, 0), (0, 0), (0, 2), (0, 0)))
        gqk = jnp.stack([jnp.tile(att_q_norm_g[i], 2), jnp.tile(att_k_norm_g[i], 2)])
        q, kt, v, rq, rk, rv, gr, xu, gu = _inproj(ctx_src, lat_src, s, mod8, norm1_g[i][None],
                                                   w_in[i].astype(BF16), rope, gqk, bd, n_ctx_tiles)
        tile_off = 0 if need_ctx else n_ctx_tiles
        tok_off = tile_off * TOK_TILE
        lam_init = 0.8 - 0.6 * math.exp(-0.3 * i)
        lam_p = jnp.pad(att_lambda[i], ((0, 0), (0, LANES - ATT_QK)))
        att = _attention(lam_p, q, kt, v, att_subln_g[i][None], ctx_len, tile_off, n_ctx_tiles, lam_init)
        ret = _retention(ret_log_decay[i], rq, rk, rv, gr, jnp.tile(ret_norm_g[i], RET_HEADS)[None], bd, ctx_len,
                         tok_off)
        gw = lru_gate_w[i]
        gate_wd = jnp.zeros((2, 2, LRU_W, LRU_W), F32)
        bw = gw.shape[-1]
        for n in range(gw.shape[2]):
            gate_wd = gate_wd.at[:, :, n * bw:(n + 1) * bw, n * bw:(n + 1) * bw].set(gw[:, :, n])
        lru = _rglru(xu, gu, lru_conv_w[i], lru_conv_b[i][None], gate_wd.astype(BF16),
                     lru_gate_b[i].reshape(4, LRU_W), lru_lambda[i], lru_norm_g[i][None], ctx_len, tok_off)
        rw = jnp.pad(router_w[i], ((0, 0), (0, LANES - N_EXPERTS)))
        rw_hi = rw.astype(BF16)
        router2 = jnp.concatenate([rw_hi, (rw - rw_hi.astype(F32)).astype(BF16)], axis=1)
        xmid, h2, probs_t = _outproj(att, ret, lru, ctx_src, lat_src, mod8, norm2_g[i][None],
                                     w_out[i].astype(BF16), router2, tile_off, n_ctx_tiles)
        segs = []
        if need_ctx:
            segs.append((0, ctx_len, EC_CAPACITY * ctx_len // N_EXPERTS))
        segs.append((ctx_len - tok_off, lat_len, EC_CAPACITY * lat_len // N_EXPERTS))
        pos, aff, win0, pages = _topk(probs_t, tuple(segs))
        xg = _moe_gather(win0, pages, pos, h2, sum(k for _, _, k in segs))
        y = _moe_ffn(xg, exp_w_gate, exp_w_up, exp_w_down, i)
        out = _moe_combine(win0, pages, pos, aff, y, xmid, mod8, tile_off, n_ctx_tiles)
        ctx_src = lat_src = out
    return out
```

```python
import functools
import math

import jax
import jax.numpy as jnp
import numpy as np
from jax import lax
from jax.experimental import pallas as pl
from jax.experimental.pallas import tpu as pltpu

F32 = jnp.float32
BF16 = jnp.bfloat16
I32 = jnp.int32

ATT_HEADS = 4
ATT_QK = 64
ATT_V = 2 * ATT_QK
ATT_W = ATT_HEADS * ATT_V
RET_HEADS = 4
RET_QK = 64
RET_W = 256
LRU_W = 256
LRU_C = 8.0
N_EXPERTS = 16
EC_CAPACITY = 2
EPS = 1e-6
ROPE_BASE = 10000.0
GRID_W = 64
IN_COLS = 3072
Q_SCALE = ATT_QK ** -0.5 * math.log2(math.e)

LANES = 128
SUBLANES = 8
TOK_TILE = 256
KEY_BLOCK = 512
CHUNK = 128
RET_CHUNK = 256
TOK_BLOCK = 256
SLOT_WIN = 64
SLOT_ALIGN = 16
VMEM_LIMIT = 56 * 1024 * 1024


def _cparams(sem, vmem=None):
    return pltpu.CompilerParams(dimension_semantics=sem, vmem_limit_bytes=vmem)


def _sigmoid(x):
    return 0.5 * (jnp.tanh(0.5 * x) + 1.0)


def _silu(x):
    return x * _sigmoid(x)


def _group_mean_sq(t, bd):
    sq = t * t
    hi = sq.astype(BF16)
    lo = (sq - hi.astype(F32)).astype(BF16)
    ss = jnp.dot(hi, bd, preferred_element_type=F32) + jnp.dot(lo, bd, preferred_element_type=F32)
    return ss * (1.0 / 64.0)


def _mod_kernel(c_ref, w_ref, b_ref, o_ref):
    s = _silu(c_ref[...]).astype(BF16)
    o_ref[...] = jnp.dot(s, w_ref[...].astype(BF16), preferred_element_type=F32) + b_ref[...]


def _modulation(cc, mod_w, mod_b):
    depth, d, d6 = mod_w.shape
    rows = cc.shape[0]
    return pl.pallas_call(
        _mod_kernel,
        grid=(depth, d6 // d),
        in_specs=[pl.BlockSpec((rows, d), lambda l, j: (0, 0)),
                  pl.BlockSpec((None, d, d), lambda l, j: (l, 0, j)),
                  pl.BlockSpec((None, 1, d), lambda l, j: (l, 0, j))],
        out_specs=pl.BlockSpec((None, rows, d), lambda l, j: (l, 0, j)),
        out_shape=jax.ShapeDtypeStruct((depth, rows, d6), F32),
        compiler_params=_cparams(("arbitrary", "arbitrary")),
        name="modulation",
    )(cc, mod_w, mod_b.reshape(depth, 1, d6))


def _samples_per_step(b):
    return 2 if b % 2 == 0 else 1


def _token_sources(ctx_src, lat_src, tile_off, n_ctx_tiles, d, sb):
    lat_base = n_ctx_tiles if lat_src is ctx_src else 0
    ctx_spec = pl.BlockSpec((sb, TOK_TILE, d), lambda bi, i, *_: (bi, jnp.minimum(i + tile_off, n_ctx_tiles - 1), 0))
    lat_spec = pl.BlockSpec((sb, TOK_TILE, d),
                            lambda bi, i, *_: (bi, jnp.maximum(i + tile_off - n_ctx_tiles, 0) + lat_base, 0))
    return ctx_spec, lat_spec


def _put_rows(ref, val, cols=slice(None)):
    rows = ref.shape[1]
    for si in range(ref.shape[0]):
        ref[si, :, cols] = val[si * rows:(si + 1) * rows].astype(ref.dtype)


def _inproj_kernel(xc_ref, xl_ref, mod_ref, ng_ref, w_ref, rope_ref, gqk_ref, bd_ref,
                   q_ref, kt_ref, v_ref, rq_ref, rk_ref, rv_ref, gr_ref, xu_ref, gu_ref, *, n_ctx_tiles):
    sb, tm, _ = xc_ref.shape
    is_ctx = pl.program_id(1) < n_ctx_tiles
    hs = []
    for si in range(sb):
        x = jnp.where(is_ctx, xc_ref[si], xl_ref[si])
        ms = jnp.mean(x * x, axis=-1, keepdims=True)
        y = x * lax.rsqrt(ms + EPS) * ng_ref[...]
        hs.append((y * (1.0 + mod_ref[si, 1:2, :]) + mod_ref[si, 0:1, :]).astype(BF16))
    hb = jnp.concatenate(hs, axis=0)

    def proj(c0, c1):
        return jnp.dot(hb, w_ref[:, c0:c1], preferred_element_type=F32)

    bd = bd_ref[...]
    cosr = rope_ref[0]
    sin_up = rope_ref[1]
    sin_dn = rope_ref[2]

    def norm_rope(t, ms, g):
        tn = t * lax.rsqrt(ms + EPS) * g
        return tn * cosr + pltpu.roll(tn, LANES - 16, 1) * sin_up + pltpu.roll(tn, 16, 1) * sin_dn

    tqk = proj(0, 2 * ATT_W)
    for hd in range(ATT_HEADS):
        c0 = hd * ATT_V
        tq = tqk[:, c0:c0 + ATT_V]
        tk = tqk[:, ATT_W + c0:ATT_W + c0 + ATT_V]
        ms = _group_mean_sq(jnp.concatenate([tq, tk], axis=1), bd)
        for si in range(sb):
            rows = slice(si * tm, (si + 1) * tm)
            q_ref[si, :, c0:c0 + ATT_V] = (norm_rope(tq[rows], ms[rows, :ATT_V], gqk_ref[0:1, :])
                                           * Q_SCALE).astype(BF16)
            kt_ref[si, c0:c0 + ATT_V, :] = norm_rope(tk[rows], ms[rows, ATT_V:], gqk_ref[1:2, :]).T.astype(BF16)
    _put_rows(v_ref, proj(2 * ATT_W, 3 * ATT_W))
    base = 3 * ATT_W
    _put_rows(rq_ref, proj(base, base + RET_W))
    _put_rows(rk_ref, proj(base + RET_W, base + 2 * RET_W) * (RET_QK ** -0.5))
    _put_rows(rv_ref, proj(base + 2 * RET_W, base + 3 * RET_W))
    _put_rows(gr_ref, proj(base + 3 * RET_W, base + 4 * RET_W))
    _put_rows(xu_ref, proj(base + 4 * RET_W, base + 4 * RET_W + LRU_W))
    _put_rows(gu_ref, proj(base + 4 * RET_W + LRU_W, base + 4 * RET_W + 2 * LRU_W))


def _inproj(ctx_src, lat_src, s, mod8, norm_g, w_in_b, rope, gqk, bd, n_ctx_tiles):
    b, _, d = ctx_src.shape
    tm = TOK_TILE
    sb = _samples_per_step(b)
    seg = lambda i: jnp.minimum(i // n_ctx_tiles, 1)
    tok = lambda w: pl.BlockSpec((sb, tm, w), lambda bi, i: (bi, i, 0))
    f32o = lambda w: jax.ShapeDtypeStruct((b, s, w), F32)
    ctx_spec, lat_spec = _token_sources(ctx_src, lat_src, 0, n_ctx_tiles, d, sb)
    return pl.pallas_call(
        functools.partial(_inproj_kernel, n_ctx_tiles=n_ctx_tiles),
        grid=(b // sb, s // tm),
        in_specs=[ctx_spec, lat_spec,
                  pl.BlockSpec((sb, None, 8, d), lambda bi, i: (bi, seg(i), 0, 0)),
                  pl.BlockSpec((1, d), lambda bi, i: (0, 0)),
                  pl.BlockSpec((d, IN_COLS), lambda bi, i: (0, 0)),
                  pl.BlockSpec((3, tm, LANES), lambda bi, i: (0, i, 0)),
                  pl.BlockSpec((2, LANES), lambda bi, i: (0, 0)),
                  pl.BlockSpec((2 * LANES, 2 * LANES), lambda bi, i: (0, 0))],
        out_specs=[tok(ATT_W),
                   pl.BlockSpec((sb, ATT_W, tm), lambda bi, i: (bi, 0, i)),
                   tok(ATT_W), tok(RET_W), tok(RET_W), tok(RET_W), tok(RET_W), tok(LRU_W), tok(LRU_W)],
        out_shape=[jax.ShapeDtypeStruct((b, s, ATT_W), BF16),
                   jax.ShapeDtypeStruct((b, ATT_W, s), BF16),
                   jax.ShapeDtypeStruct((b, s, ATT_W), BF16),
                   f32o(RET_W), f32o(RET_W), f32o(RET_W), f32o(RET_W), f32o(LRU_W), f32o(LRU_W)],
        compiler_params=_cparams(("arbitrary", "arbitrary"), VMEM_LIMIT),
        name="inproj",
    )(ctx_src, lat_src, mod8, norm_g, w_in_b, rope, gqk, bd)


def _attn_kernel(lam_ref, q_ref, kt_ref, v_ref, g_ref, o_ref, *, ctx_len, q_off, n_ctx_tiles, lam_init):
    seq_len = kt_ref.shape[1]
    lp = lam_ref[...]
    lam = (jnp.exp(jnp.sum(lp[0:1, :] * lp[1:2, :], axis=1, keepdims=True))
           - jnp.exp(jnp.sum(lp[2:3, :] * lp[3:4, :], axis=1, keepdims=True)) + lam_init)
    lane = lax.broadcasted_iota(I32, (q_ref.shape[0], ATT_V), 1)

    def head(h, carry):
        cols = pl.ds(pl.multiple_of(h * ATT_V, ATT_V), ATT_V)
        q = q_ref[:, cols]

        def attend(n_keys):
            outs = []
            for mi in range(2):
                keep = (lane < ATT_QK) if mi == 0 else (lane >= ATT_QK)
                qm = jnp.where(keep, q, jnp.zeros_like(q))
                s = jnp.dot(qm, kt_ref[cols, 0:n_keys], preferred_element_type=F32)
                p = jnp.exp2(s - jnp.max(s, axis=1, keepdims=True))
                outs.append((p, jnp.sum(p, axis=1, keepdims=True)))
            a = outs[0][0] * (1.0 / outs[0][1]) - outs[1][0] * (lam / outs[1][1])
            o = jnp.dot(a.astype(BF16), v_ref[0:n_keys, cols], preferred_element_type=F32)
            o = o * lax.rsqrt(jnp.mean(o * o, axis=-1, keepdims=True) + EPS) * g_ref[...] * (1.0 - lam_init)
            o_ref[:, cols] = o.astype(BF16)

        if q_off >= n_ctx_tiles:
            attend(seq_len)
        else:
            is_ctx = pl.program_id(1) + q_off < n_ctx_tiles
            pl.when(is_ctx)(lambda: attend(ctx_len))
            pl.when(jnp.logical_not(is_ctx))(lambda: attend(seq_len))
        return carry

    lax.fori_loop(0, ATT_HEADS, head, 0)


def _attention(lam_p, q, kt, v, subln_g, ctx_len, q_off, n_ctx_tiles, lam_init):
    b, s, _ = q.shape
    tq = TOK_TILE
    nq = s // tq - q_off
    kern = functools.partial(_attn_kernel, ctx_len=ctx_len, q_off=q_off, n_ctx_tiles=n_ctx_tiles,
                             lam_init=lam_init)
    return pl.pallas_call(
        kern,
        grid=(b, nq),
        in_specs=[pl.BlockSpec((4, LANES), lambda bi, i: (0, 0)),
                  pl.BlockSpec((None, tq, ATT_W), lambda bi, i: (bi, i + q_off, 0)),
                  pl.BlockSpec((None, ATT_W, s), lambda bi, i: (bi, 0, 0)),
                  pl.BlockSpec((None, s, ATT_W), lambda bi, i: (bi, 0, 0)),
                  pl.BlockSpec((1, ATT_V), lambda bi, i: (0, 0))],
        out_specs=pl.BlockSpec((None, tq, ATT_W), lambda bi, i: (bi, i, 0)),
        out_shape=jax.ShapeDtypeStruct((b, nq * tq, ATT_W), BF16),
        compiler_params=_cparams(("arbitrary", "arbitrary"), VMEM_LIMIT),
        name="diff_attention",
    )(lam_p, q, kt, v, subln_g)


def _ret_kernel(lg_ref, q_ref, k_ref, v_ref, gr_ref, g_ref, bd_ref, o_ref,
                of_ref, ob_ref, st_ref, dm_ref, cr_ref, in_ref, cd_ref, *, n_ctx_chunks, n_chunks, out_chunk0):
    c = RET_CHUNK
    rown = lax.broadcasted_iota(I32, (c, c), 0)
    colm = lax.broadcasted_iota(I32, (c, c), 1)
    low = lax.broadcasted_iota(I32, (c, LANES), 1) < RET_QK
    rowf = lax.broadcasted_iota(I32, (c, LANES), 0).astype(F32)
    srow = lax.broadcasted_iota(I32, (LANES, LANES), 0)
    scol = lax.broadcasted_iota(I32, (LANES, LANES), 1)
    for d in range(2):
        if d == 0:
            diff = (rown - colm).astype(F32)
            mask = rown >= colm
            cross_pw = rowf + 1.0
            inner_pw = (c - 1.0) - rowf
        else:
            diff = (colm - rown).astype(F32)
            mask = colm > rown
            cross_pw = c - rowf
            inner_pw = rowf
        for hp in range(2):
            dm_ref[d, hp] = jnp.concatenate(
                [jnp.where(mask, jnp.exp(lg_ref[d, 2 * hp + j] * jnp.where(mask, diff, 0.0)), 0.0) for j in range(2)],
                axis=1)
            lg_lane = jnp.where(low, lg_ref[d, 2 * hp], lg_ref[d, 2 * hp + 1])
            cr_ref[d, hp] = jnp.exp(lg_lane * cross_pw)
            in_ref[d, hp] = jnp.exp(lg_lane * inner_pw)
            cd_ref[d, hp] = jnp.exp(jnp.where(srow < RET_QK, lg_ref[d, 2 * hp], lg_ref[d, 2 * hp + 1]) * float(c))
    st_ref[...] = jnp.zeros(st_ref.shape, F32)
    bdmask = (srow < RET_QK) == (scol < RET_QK)

    def step(i, carry):
        for d in range(2):
            if d == 0:
                ci = i
            else:
                ci = jnp.where(i < n_ctx_chunks, n_ctx_chunks - 1 - i, n_chunks - 1 - (i - n_ctx_chunks))
            r0 = pl.multiple_of(ci * c, c)
            dst = of_ref if d == 0 else ob_ref
            for hp in range(2):
                cols = slice(hp * LANES, (hp + 1) * LANES)
                kf = k_ref[pl.ds(r0, c), cols]
                qf = q_ref[pl.ds(r0, c), cols]
                kb = kf.astype(BF16)
                vb = v_ref[pl.ds(r0, c), cols].astype(BF16)
                zero = jnp.zeros_like(kb)
                nt = (((1,), (1,)), ((), ()))
                kcat = jnp.concatenate([jnp.where(low, kb, zero), jnp.where(low, zero, kb)], axis=0)
                sc = (lax.dot_general(qf.astype(BF16), kcat, nt, preferred_element_type=F32)
                      * dm_ref[d, hp]).astype(BF16)
                vbd = jnp.concatenate([jnp.where(low, vb, zero), jnp.where(low, zero, vb)], axis=0)
                st = st_ref[d, hp]
                intra = jnp.dot(sc, vbd, preferred_element_type=F32)
                inter = jnp.dot(qf.astype(BF16), st.astype(BF16), preferred_element_type=F32) * cr_ref[d, hp]
                dst[pl.ds(r0, c), cols] = intra + inter
                kin = (kf * in_ref[d, hp]).astype(BF16)
                kv = lax.dot_general(kin, vb, (((0,), (0,)), ((), ())), preferred_element_type=F32)
                st_ref[d, hp] = st * cd_ref[d, hp] + jnp.where(bdmask, kv, 0.0)
        return carry

    lax.fori_loop(0, n_chunks, step, 0, unroll=2)

    bd = bd_ref[...]

    def finish(i, carry):
        r0 = pl.multiple_of((i + out_chunk0) * c, c)
        ro = pl.multiple_of(i * c, c)
        o = of_ref[pl.ds(r0, c), :] + ob_ref[pl.ds(r0, c), :]
        on = o * lax.rsqrt(_group_mean_sq(o, bd) + EPS) * g_ref[...]
        o_ref[pl.ds(ro, c), :] = (on * _silu(gr_ref[pl.ds(r0, c), :])).astype(BF16)
        return carry

    lax.fori_loop(0, n_chunks - out_chunk0, finish, 0)


def _retention(log_decay, rq, rk, rv, gr, norm_g, bd, ctx_len, out_off):
    b, s, w = rq.shape
    c = RET_CHUNK
    assert ctx_len % c == 0 and s % c == 0 and out_off % c == 0 and w == 2 * LANES
    n_chunks = s // c
    out_chunk0 = out_off // c
    kern = functools.partial(_ret_kernel, n_ctx_chunks=ctx_len // c, n_chunks=n_chunks, out_chunk0=out_chunk0)
    seq = pl.BlockSpec((None, s, w), lambda bi: (bi, 0, 0))
    state = pltpu.VMEM((2, 2, LANES, LANES), F32)
    rows = pltpu.VMEM((2, 2, c, LANES), F32)
    return pl.pallas_call(
        kern,
        grid=(b,),
        in_specs=[pl.BlockSpec(memory_space=pltpu.SMEM), seq, seq, seq, seq,
                  pl.BlockSpec((1, w), lambda bi: (0, 0)),
                  pl.BlockSpec((2 * LANES, 2 * LANES), lambda bi: (0, 0))],
        out_specs=pl.BlockSpec((None, s - out_off, w), lambda bi: (bi, 0, 0)),
        out_shape=jax.ShapeDtypeStruct((b, s - out_off, w), BF16),
        scratch_shapes=[pltpu.VMEM((s, w), F32), pltpu.VMEM((s, w), F32), state,
                        pltpu.VMEM((2, 2, c, 2 * c), F32), rows, rows, state],
        compiler_params=_cparams(("arbitrary",), VMEM_LIMIT),
        name="retention",
    )(log_decay, rq, rk, rv, gr, norm_g, bd)


def _gelu_tanh(x):
    return 0.5 * x * (1.0 + jnp.tanh(math.sqrt(2.0 / math.pi) * (x + 0.044715 * (x * x * x))))


def _lru_kernel(xu_ref, gu_ref, cw_ref, cb_ref, gw_ref, gb_ref, lam_ref, ng_ref, o_ref,
                u_ref, hf_ref, hb_ref, *, ctx_len, seq_len, out_chunk0):
    c = CHUNK
    n_chunks = seq_len // c
    n_ctx_chunks = ctx_len // c
    row = lax.broadcasted_iota(I32, (c, LRU_W), 0)

    def conv(i, carry):
        r0 = pl.multiple_of(i * c, c)
        prev0 = pl.multiple_of(jnp.maximum(r0 - 8, 0), 8)
        next0 = pl.multiple_of(jnp.minimum(r0 + c, seq_len - 8), 8)
        ext = jnp.concatenate([xu_ref[pl.ds(prev0, 8), :], xu_ref[pl.ds(r0, c), :], xu_ref[pl.ds(next0, 8), :]],
                              axis=0)
        seg_first = jnp.logical_or(i == 0, i == n_ctx_chunks)
        seg_last = jnp.logical_or(i == n_ctx_chunks - 1, i == n_chunks - 1)
        xm2 = pltpu.roll(ext, 2, 0)[8:8 + c]
        xm1 = pltpu.roll(ext, 1, 0)[8:8 + c]
        xp1 = pltpu.roll(ext, c + 16 - 1, 0)[8:8 + c]
        xm2 = jnp.where(jnp.logical_and(seg_first, row < 2), 0.0, xm2)
        xm1 = jnp.where(jnp.logical_and(seg_first, row < 1), 0.0, xm1)
        xp1 = jnp.where(jnp.logical_and(seg_last, row >= c - 1), 0.0, xp1)
        u_ref[pl.ds(r0, c), :] = (cw_ref[0:1, :] * xm2 + cw_ref[1:2, :] * xm1 + cw_ref[2:3, :] * ext[8:8 + c]
                                  + cw_ref[3:4, :] * xp1 + cb_ref[...])
        return carry

    lax.fori_loop(0, n_chunks, conv, 0)

    def softplus(z):
        return jnp.maximum(z, 0.0) + jnp.log1p(jnp.exp(-jnp.abs(z)))

    def scan_dir(d, u):
        ub = u.astype(BF16)
        zr = jnp.dot(ub, gw_ref[d, 0], preferred_element_type=F32) + gb_ref[2 * d:2 * d + 1, :]
        zi = jnp.dot(ub, gw_ref[d, 1], preferred_element_type=F32) + gb_ref[2 * d + 1:2 * d + 2, :]
        log_a = -LRU_C * softplus(-lam_ref[d:d + 1, :]) * _sigmoid(zr)
        a = jnp.exp(log_a)
        b = jnp.sqrt(-jnp.tanh(log_a) * (a * a + 1.0)) * (_sigmoid(zi) * u)
        ng = c // SUBLANES
        a = a.reshape(ng, SUBLANES, LRU_W)
        b = b.reshape(ng, SUBLANES, LRU_W)
        sub = lax.broadcasted_iota(I32, (ng, SUBLANES, LRU_W), 1)
        sh = 1
        while sh < SUBLANES:
            if d == 0:
                valid = sub >= sh
                a_prev = pltpu.roll(a, sh, 1)
                b_prev = pltpu.roll(b, sh, 1)
            else:
                valid = sub < SUBLANES - sh
                a_prev = pltpu.roll(a, SUBLANES - sh, 1)
                b_prev = pltpu.roll(b, SUBLANES - sh, 1)
            b = a * jnp.where(valid, b_prev, 0.0) + b
            a = a * jnp.where(valid, a_prev, 1.0)
            sh *= 2
        return a, b

    def chain(d, a, b, h):
        ng = c // SUBLANES
        out = [None] * ng
        for g in (range(ng) if d == 0 else range(ng - 1, -1, -1)):
            hg = a[g] * h + b[g]
            out[g] = hg
            h = hg[SUBLANES - 1:SUBLANES, :] if d == 0 else hg[0:1, :]
        return jnp.concatenate(out, axis=0), h

    def step(i, carry):
        h_f, h_b = carry
        r0 = pl.multiple_of(i * c, c)
        a, b = scan_dir(0, u_ref[pl.ds(r0, c), :])
        hh, h_f = chain(0, a, b, h_f)
        hf_ref[pl.ds(r0, c), :] = hh
        cb = jnp.where(i < n_ctx_chunks, n_ctx_chunks - 1 - i, n_chunks - 1 - (i - n_ctx_chunks))
        rb = pl.multiple_of(cb * c, c)
        a, b = scan_dir(1, u_ref[pl.ds(rb, c), :])
        hh, h_b = chain(1, a, b, h_b)
        hb_ref[pl.ds(rb, c), :] = hh
        return h_f, h_b

    zero = jnp.zeros((1, LRU_W), F32)
    lax.fori_loop(0, n_chunks, step, (zero, zero))

    def finish(i, carry):
        r0 = pl.multiple_of((i + out_chunk0) * c, c)
        ro = pl.multiple_of(i * c, c)
        t = (hf_ref[pl.ds(r0, c), :] + hb_ref[pl.ds(r0, c), :]) * _gelu_tanh(gu_ref[pl.ds(r0, c), :])
        t = t * lax.rsqrt(jnp.mean(t * t, axis=-1, keepdims=True) + EPS) * ng_ref[...]
        o_ref[pl.ds(ro, c), :] = t.astype(BF16)
        return carry

    lax.fori_loop(0, n_chunks - out_chunk0, finish, 0)


def _rglru(xu, gu, conv_w, conv_b, gate_wd, gate_b, lam, norm_g, ctx_len, out_off):
    b, s, w = xu.shape
    kern = functools.partial(_lru_kernel, ctx_len=ctx_len, seq_len=s, out_chunk0=out_off // CHUNK)
    seq = pl.BlockSpec((None, s, w), lambda bi: (bi, 0, 0))
    full = lambda shp: pl.BlockSpec(shp, lambda bi: (0,) * len(shp))
    return pl.pallas_call(
        kern,
        grid=(b,),
        in_specs=[seq, seq, full((4, w)), full((1, w)), full((2, 2, w, w)), full((4, w)), full((2, w)),
                  full((1, w))],
        out_specs=pl.BlockSpec((None, s - out_off, w), lambda bi: (bi, 0, 0)),
        out_shape=jax.ShapeDtypeStruct((b, s - out_off, w), BF16),
        scratch_shapes=[pltpu.VMEM((s, w), F32), pltpu.VMEM((s, w), F32), pltpu.VMEM((s, w), F32)],
        compiler_params=_cparams(("arbitrary",), VMEM_LIMIT),
        name="rglru",
    )(xu, gu, conv_w, conv_b, gate_wd, gate_b, lam, norm_g)


def _outproj_kernel(att_ref, ret_ref, lru_ref, xc_ref, xl_ref, mod_ref, ng_ref, w_ref, rw_ref,
                    xmid_ref, h2_ref, pt_ref, *, tile_off, n_ctx_tiles):
    sb, tm, _ = att_ref.shape
    mix = jnp.concatenate([jnp.concatenate([att_ref[si], ret_ref[si], lru_ref[si]], axis=1) for si in range(sb)],
                          axis=0)
    y = jnp.dot(mix, w_ref[...], preferred_element_type=F32)
    is_ctx = pl.program_id(1) + tile_off < n_ctx_tiles
    hs = []
    for si in range(sb):
        x = jnp.where(is_ctx, xc_ref[si], xl_ref[si]) + mod_ref[si, 2:3, :] * y[si * tm:(si + 1) * tm]
        xmid_ref[si] = x
        ms = jnp.mean(x * x, axis=-1, keepdims=True)
        hs.append((x * lax.rsqrt(ms + EPS) * ng_ref[...]) * (1.0 + mod_ref[si, 4:5, :]) + mod_ref[si, 3:4, :])
    h = jnp.concatenate(hs, axis=0)
    hb = h.astype(BF16)
    _put_rows(h2_ref, hb)
    hlo = (h - hb.astype(F32)).astype(BF16)
    wide = jnp.dot(hb, rw_ref[...], preferred_element_type=F32)
    logits = (wide[:, :LANES] + wide[:, LANES:]
              + jnp.dot(hlo, rw_ref[:, :LANES], preferred_element_type=F32))
    lane = lax.broadcasted_iota(I32, logits.shape, 1)
    logits = jnp.where(lane < N_EXPERTS, logits, -jnp.inf)
    e = jnp.exp(logits - jnp.max(logits, axis=-1, keepdims=True))
    probs = e / jnp.sum(e, axis=-1, keepdims=True)
    for si in range(sb):
        pt_ref[si] = probs[si * tm:(si + 1) * tm].T[0:N_EXPERTS, :]


def _outproj(att, ret, lru, ctx_src, lat_src, mod8, norm_g, w_out_b, router2, tile_off, n_ctx_tiles):
    b, so, _ = att.shape
    d = ctx_src.shape[-1]
    tm = TOK_TILE
    sb = 1
    seg = lambda i: jnp.minimum((i + tile_off) // n_ctx_tiles, 1)
    tok = lambda w: pl.BlockSpec((sb, tm, w), lambda bi, i: (bi, i, 0))
    ctx_spec, lat_spec = _token_sources(ctx_src, lat_src, tile_off, n_ctx_tiles, d, sb)
    return pl.pallas_call(
        functools.partial(_outproj_kernel, tile_off=tile_off, n_ctx_tiles=n_ctx_tiles),
        grid=(b // sb, so // tm),
        in_specs=[tok(ATT_W), tok(RET_W), tok(LRU_W), ctx_spec, lat_spec,
                  pl.BlockSpec((sb, None, 8, d), lambda bi, i: (bi, seg(i), 0, 0)),
                  pl.BlockSpec((1, d), lambda bi, i: (0, 0)),
                  pl.BlockSpec((d, d), lambda bi, i: (0, 0)),
                  pl.BlockSpec((d, 2 * LANES), lambda bi, i: (0, 0))],
        out_specs=[tok(d), tok(d), pl.BlockSpec((sb, N_EXPERTS, tm), lambda bi, i: (bi, 0, i))],
        out_shape=[jax.ShapeDtypeStruct((b, so, d), F32), jax.ShapeDtypeStruct((b, so, d), BF16),
                   jax.ShapeDtypeStruct((b, N_EXPERTS, so), F32)],
        compiler_params=_cparams(("arbitrary", "arbitrary"), VMEM_LIMIT),
        name="outproj_router",
    )(att, ret, lru, ctx_src, lat_src, mod8, norm_g, w_out_b, router2)


def _topk_kernel(p_ref, pos_ref, aff_ref, win_ref, pages_ref, *, segs):
    ne = p_ref.shape[0]
    block_first_slot = []
    ri = lax.broadcasted_iota(I32, (LANES, LANES), 0)
    ci = lax.broadcasted_iota(I32, (LANES, LANES), 1)
    before = jnp.where(ri < ci, 1.0, 0.0).astype(BF16)
    run_sel = jnp.zeros((ne, 1), F32)

    def kth_floor(vals, k):
        t = jnp.zeros((ne, 1), I32)
        for bit in range(30, -1, -1):
            cand = t | (1 << bit)
            cnt = jnp.sum(jnp.where(vals >= pltpu.bitcast(cand, F32), 1.0, 0.0), axis=1, keepdims=True)
            t = jnp.where(cnt >= k, cand, t)
        return pltpu.bitcast(t, F32)

    for start, ln, k in segs:
        p = p_ref[:, start:start + ln]
        resid = p - kth_floor(p, k)
        t = kth_floor(resid, k)
        gt = resid > t
        eq = resid == t
        need = k - jnp.sum(jnp.where(gt, 1.0, 0.0), axis=1, keepdims=True)
        run_eq = jnp.zeros((ne, 1), F32)
        for j in range(ln // LANES):
            sl = slice(j * LANES, (j + 1) * LANES)
            if (start + j * LANES) % TOK_BLOCK == 0:
                block_first_slot.append(run_sel)
            eq_t = jnp.where(eq[:, sl], 1.0, 0.0)
            rank_eq = jnp.dot(eq_t.astype(BF16), before, preferred_element_type=F32) + run_eq
            sel = jnp.logical_or(gt[:, sl], jnp.logical_and(eq[:, sl], rank_eq < need))
            sel_t = jnp.where(sel, 1.0, 0.0)
            slot = jnp.dot(sel_t.astype(BF16), before, preferred_element_type=F32) + run_sel
            pos_ref[:, start + j * LANES:start + (j + 1) * LANES] = jnp.where(sel, slot, -1.0).astype(I32)
            aff_ref[:, start + j * LANES:start + (j + 1) * LANES] = jnp.where(sel, p[:, sl], 0.0)
            run_eq = run_eq + jnp.sum(eq_t, axis=1, keepdims=True)
            run_sel = run_sel + jnp.sum(sel_t, axis=1, keepdims=True)

    block_first_slot.append(run_sel)
    lane = lax.broadcasted_iota(I32, (ne, LANES), 1)
    win0 = jnp.zeros((ne, LANES), I32)
    pages = jnp.zeros((ne, LANES), I32)
    for j in range(len(block_first_slot) - 1):
        first = block_first_slot[j].astype(I32)
        end = block_first_slot[j + 1].astype(I32)
        w0 = jnp.left_shift(jnp.right_shift(first, SLOT_ALIGN.bit_length() - 1), SLOT_ALIGN.bit_length() - 1)
        need_pages = jnp.right_shift(end - w0 + (SLOT_WIN - 1), SLOT_WIN.bit_length() - 1)
        per_sample = jnp.max(need_pages.astype(F32).reshape(ne // N_EXPERTS, N_EXPERTS, 1), axis=1, keepdims=True)
        need_pages = jnp.broadcast_to(per_sample, (ne // N_EXPERTS, N_EXPERTS, 1)).reshape(ne, 1).astype(I32)
        win0 = jnp.where(lane == j, w0, win0)
        pages = jnp.where(lane == j, need_pages, pages)
    win_ref[...] = win0
    pages_ref[...] = pages


def _topk(probs_t, segs):
    b, ne, so = probs_t.shape
    assert so // TOK_BLOCK <= LANES and ne == N_EXPERTS
    rows = b * ne
    blk = pl.BlockSpec((rows, so), lambda i: (0, 0))
    tab = pl.BlockSpec((rows, LANES), lambda i: (0, 0))
    pos, aff, win0, pages = pl.pallas_call(
        functools.partial(_topk_kernel, segs=segs),
        grid=(1,),
        in_specs=[blk],
        out_specs=[blk, blk, tab, tab],
        out_shape=[jax.ShapeDtypeStruct((rows, so), I32), jax.ShapeDtypeStruct((rows, so), F32),
                   jax.ShapeDtypeStruct((rows, LANES), I32), jax.ShapeDtypeStruct((rows, LANES), I32)],
        compiler_params=_cparams(("arbitrary",), VMEM_LIMIT),
        name="expert_choice_topk",
    )(probs_t.reshape(rows, so))
    pos, aff = pos.reshape(b, ne, so), aff.reshape(b, ne, so)
    win0, pages = win0.reshape(b, ne, LANES), pages.reshape(b, ne, LANES)
    nblk = so // TOK_BLOCK
    win0 = jnp.swapaxes(win0[:, :, :nblk], 1, 2).reshape(-1)
    pages = pages[:, 0, :nblk].reshape(-1)
    return pos, aff, win0, pages


def _slot_windows(win_ref, base, page, n_slots):
    wins = []
    for e in range(N_EXPERTS):
        lo = win_ref[base + e] + page * SLOT_WIN
        wins.append((lo, pl.multiple_of(jnp.minimum(lo, n_slots - SLOT_WIN), SLOT_ALIGN)))
    return wins


def _window_hits(pos, wins):
    row = lax.broadcasted_iota(I32, (SLOT_WIN, TOK_BLOCK), 0)
    hits = []
    for e, (lo, start) in enumerate(wins):
        pe = pos[e:e + 1, :]
        hits.append(jnp.logical_and(pe - start == row, pe >= lo))
    return hits


def _moe_gather_kernel(win_ref, pages_ref, pos_ref, h_ref, xg_ref, *, n_slots):
    j = pl.program_id(1)
    blk = pl.program_id(0) * pl.num_programs(1) + j

    @pl.when(j == 0)
    def _():
        xg_ref[...] = jnp.zeros(xg_ref.shape, BF16)

    pos = pos_ref[...]

    def page(r, carry):
        wins = _slot_windows(win_ref, blk * N_EXPERTS, r, n_slots)
        onehot = jnp.concatenate([jnp.where(hit, 1.0, 0.0).astype(BF16) for hit in _window_hits(pos, wins)], axis=0)
        g = jnp.dot(onehot, h_ref[...], preferred_element_type=F32)
        for e, (_, start) in enumerate(wins):
            xg_ref[e, pl.ds(start, SLOT_WIN), :] += g[e * SLOT_WIN:(e + 1) * SLOT_WIN, :].astype(BF16)
        return carry

    lax.fori_loop(0, pages_ref[blk], page, 0)


def _moe_gather(win0, pages, pos, h2, n_slots):
    b, so, d = h2.shape
    ne = pos.shape[1]
    assert so % TOK_BLOCK == 0 and n_slots >= SLOT_WIN and n_slots % SLOT_ALIGN == 0 and ne == N_EXPERTS
    grid_spec = pltpu.PrefetchScalarGridSpec(
        num_scalar_prefetch=2,
        grid=(b, so // TOK_BLOCK),
        in_specs=[pl.BlockSpec((None, ne, TOK_BLOCK), lambda bi, j, w, p: (bi, 0, j)),
                  pl.BlockSpec((None, TOK_BLOCK, d), lambda bi, j, w, p: (bi, j, 0))],
        out_specs=pl.BlockSpec((None, ne, n_slots, d), lambda bi, j, w, p: (bi, 0, 0, 0)))
    return pl.pallas_call(
        functools.partial(_moe_gather_kernel, n_slots=n_slots),
        grid_spec=grid_spec,
        out_shape=jax.ShapeDtypeStruct((b, ne, n_slots, d), BF16),
        compiler_params=_cparams(("arbitrary", "arbitrary"), VMEM_LIMIT),
        name="moe_gather",
    )(win0, pages, pos, h2)


def _moe_ffn_kernel(x_ref, wg_ref, wu_ref, wd_ref, y_ref, wgb_ref, wub_ref, wdb_ref):
    @pl.when(pl.program_id(1) == 0)
    def _():
        wgb_ref[...] = wg_ref[...].astype(BF16)
        wub_ref[...] = wu_ref[...].astype(BF16)
        wdb_ref[...] = wd_ref[...].astype(BF16)

    xb = jnp.concatenate([x_ref[si] for si in range(x_ref.shape[0])], axis=0)
    gate = jnp.dot(xb, wgb_ref[...], preferred_element_type=F32)
    up = jnp.dot(xb, wub_ref[...], preferred_element_type=F32)
    hid = (_silu(gate) * up).astype(BF16)
    _put_rows(y_ref, jnp.dot(hid, wdb_ref[...], preferred_element_type=F32))


def _moe_ffn(xg, wg, wu, wd, layer):
    b, ne, n_slots, d = xg.shape
    ff = wg.shape[3]
    sb = _samples_per_step(b)
    rows = pl.BlockSpec((sb, None, n_slots, d), lambda e, bi: (bi, e, 0, 0))
    return pl.pallas_call(
        _moe_ffn_kernel,
        grid=(ne, b // sb),
        in_specs=[rows,
                  pl.BlockSpec((None, None, d, ff), lambda e, bi: (layer, e, 0, 0)),
                  pl.BlockSpec((None, None, d, ff), lambda e, bi: (layer, e, 0, 0)),
                  pl.BlockSpec((None, None, ff, d), lambda e, bi: (layer, e, 0, 0))],
        out_specs=rows,
        out_shape=jax.ShapeDtypeStruct((b, ne, n_slots, d), BF16),
        scratch_shapes=[pltpu.VMEM((d, ff), BF16), pltpu.VMEM((d, ff), BF16), pltpu.VMEM((ff, d), BF16)],
        compiler_params=_cparams(("arbitrary", "arbitrary"), VMEM_LIMIT),
        name="expert_ffn",
    )(xg, wg, wu, wd)


def _moe_combine_kernel(win_ref, pages_ref, pos_ref, aff_ref, y_ref, x_ref, mod_ref, o_ref, ycat_ref, acc_ref, *,
                        n_slots):
    blk = pl.program_id(0) * pl.num_programs(1) + pl.program_id(1)
    pos = pos_ref[...]
    aff = aff_ref[...]
    aff_hi = aff.astype(BF16).astype(F32)
    aff_lo = aff - aff_hi
    acc_ref[...] = jnp.zeros(acc_ref.shape, F32)
    tn = (((0,), (0,)), ((), ()))

    def page(r, carry):
        wins = _slot_windows(win_ref, blk * N_EXPERTS, r, n_slots)
        hits = _window_hits(pos, wins)
        for e, (_, start) in enumerate(wins):
            ycat_ref[e * SLOT_WIN:(e + 1) * SLOT_WIN, :] = y_ref[e, pl.ds(start, SLOT_WIN), :]
        w_hi = jnp.concatenate([jnp.where(hit, aff_hi[e:e + 1, :], 0.0).astype(BF16) for e, hit in enumerate(hits)],
                               axis=0)
        w_lo = jnp.concatenate([jnp.where(hit, aff_lo[e:e + 1, :], 0.0).astype(BF16) for e, hit in enumerate(hits)],
                               axis=0)
        ycat = ycat_ref[...]
        acc_ref[...] += (lax.dot_general(w_hi, ycat, tn, preferred_element_type=F32)
                         + lax.dot_general(w_lo, ycat, tn, preferred_element_type=F32))
        return carry

    lax.fori_loop(0, pages_ref[blk], page, 0)
    o_ref[...] = x_ref[...] + mod_ref[5:6, :] * acc_ref[...]


def _moe_combine(win0, pages, pos, aff, y, xmid, mod8, tile_off, n_ctx_tiles):
    b, so, d = xmid.shape
    _, ne, n_slots, _ = y.shape
    assert TOK_BLOCK == TOK_TILE
    if n_ctx_tiles and tile_off < n_ctx_tiles:
        seg = lambda i: jnp.minimum((i + tile_off) // n_ctx_tiles, 1)
    else:
        seg = lambda i: 1
    sel = pl.BlockSpec((None, ne, TOK_BLOCK), lambda bi, j, w, p: (bi, 0, j))
    tok = pl.BlockSpec((None, TOK_BLOCK, d), lambda bi, j, w, p: (bi, j, 0))
    grid_spec = pltpu.PrefetchScalarGridSpec(
        num_scalar_prefetch=2,
        grid=(b, so // TOK_BLOCK),
        in_specs=[sel, sel,
                  pl.BlockSpec((None, ne, n_slots, d), lambda bi, j, w, p: (bi, 0, 0, 0)),
                  tok,
                  pl.BlockSpec((None, None, 8, d), lambda bi, j, w, p: (bi, seg(j), 0, 0))],
        out_specs=tok,
        scratch_shapes=[pltpu.VMEM((ne * SLOT_WIN, d), BF16), pltpu.VMEM((TOK_BLOCK, d), F32)])
    return pl.pallas_call(
        functools.partial(_moe_combine_kernel, n_slots=n_slots),
        grid_spec=grid_spec,
        out_shape=jax.ShapeDtypeStruct((b, so, d), F32),
        compiler_params=_cparams(("arbitrary", "arbitrary"), VMEM_LIMIT),
        name="moe_combine",
    )(win0, pages, pos, aff, y, xmid, mod8)


def _rope_tables(ctx_len, lat_len):
    rows = lat_len // GRID_W
    row = jnp.repeat(jnp.arange(rows), GRID_W).astype(F32)
    col = jnp.tile(jnp.arange(GRID_W), rows).astype(F32)
    axis_dim = ATT_QK // 2
    inv = 1.0 / (ROPE_BASE ** (jnp.arange(0, axis_dim, 2, dtype=F32) / axis_dim))
    ang_row = row[:, None] * inv
    ang_col = col[:, None] * inv
    z = jnp.zeros_like(ang_row)
    cos64 = jnp.concatenate([jnp.cos(ang_row), jnp.cos(ang_row), jnp.cos(ang_col), jnp.cos(ang_col)], axis=-1)
    up64 = jnp.concatenate([-jnp.sin(ang_row), z, -jnp.sin(ang_col), z], axis=-1)
    dn64 = jnp.concatenate([z, jnp.sin(ang_row), z, jnp.sin(ang_col)], axis=-1)
    lat = jnp.stack([jnp.tile(t, (1, 2)) for t in (cos64, up64, dn64)])
    ctx = jnp.stack([jnp.ones((ctx_len, LANES), F32), jnp.zeros((ctx_len, LANES), F32),
                     jnp.zeros((ctx_len, LANES), F32)])
    return jnp.concatenate([ctx, lat], axis=1)


def kernel(x, c, ctx, c_ctx, mod_w, mod_b, norm1_g, norm2_g, w_in, w_out, att_q_norm_g, att_k_norm_g, att_lambda,
           att_subln_g, ret_log_decay, ret_norm_g, lru_conv_w, lru_conv_b, lru_gate_w, lru_gate_b, lru_lambda,
           lru_norm_g, router_w, exp_w_gate, exp_w_up, exp_w_down):
    b, lat_len, d = x.shape
    ctx_len = ctx.shape[1]
    depth = mod_w.shape[0]
    s = ctx_len + lat_len
    assert ctx_len % TOK_TILE == 0 and lat_len % KEY_BLOCK == 0 and lat_len % GRID_W == 0
    n_ctx_tiles = ctx_len // TOK_TILE

    rows = -(-(b + 1) // 8) * 8
    cc = jnp.zeros((rows, d), F32).at[:b].set(c).at[b].set(c_ctx)
    mod_all = _modulation(cc, mod_w, mod_b).reshape(depth, rows, 6, d)

    rope = _rope_tables(ctx_len, lat_len)
    lane = np.arange(2 * LANES)
    bd = jnp.asarray((lane[:, None] // 64) == (lane[None, :] // 64), BF16)
    ctx_src, lat_src = ctx, x

    out = None
    for i in range(depth):
        need_ctx = i < depth - 1
        m6 = mod_all[i]
        mod8 = jnp.stack([jnp.broadcast_to(m6[b], (b, 6, d)), m6[:b]], axis=1)
        mod8 = jnp.pad(mod8, ((0, 0), (0, 0), (0, 2), (0, 0)))
        gqk = jnp.stack([jnp.tile(att_q_norm_g[i], 2), jnp.tile(att_k_norm_g[i], 2)])
        q, kt, v, rq, rk, rv, gr, xu, gu = _inproj(ctx_src, lat_src, s, mod8, norm1_g[i][None],
                                                   w_in[i].astype(BF16), rope, gqk, bd, n_ctx_tiles)
        tile_off = 0 if need_ctx else n_ctx_tiles
        tok_off = tile_off * TOK_TILE
        lam_init = 0.8 - 0.6 * math.exp(-0.3 * i)
        lam_p = jnp.pad(att_lambda[i], ((0, 0), (0, LANES - ATT_QK)))
        att = _attention(lam_p, q, kt, v, att_subln_g[i][None], ctx_len, tile_off, n_ctx_tiles, lam_init)
        ret = _retention(ret_log_decay[i], rq, rk, rv, gr, jnp.tile(ret_norm_g[i], RET_HEADS)[None], bd, ctx_len,
                         tok_off)
        gw = lru_gate_w[i]
        gate_wd = jnp.zeros((2, 2, LRU_W, LRU_W), F32)
        bw = gw.shape[-1]
        for n in range(gw.shape[2]):
            gate_wd = gate_wd.at[:, :, n * bw:(n + 1) * bw, n * bw:(n + 1) * bw].set(gw[:, :, n])
        lru = _rglru(xu, gu, lru_conv_w[i], lru_conv_b[i][None], gate_wd.astype(BF16),
                     lru_gate_b[i].reshape(4, LRU_W), lru_lambda[i], lru_norm_g[i][None], ctx_len, tok_off)
        rw = jnp.pad(router_w[i], ((0, 0), (0, LANES - N_EXPERTS)))
        rw_hi = rw.astype(BF16)
        router2 = jnp.concatenate([rw_hi, (rw - rw_hi.astype(F32)).astype(BF16)], axis=1)
        xmid, h2, probs_t = _outproj(att, ret, lru, ctx_src, lat_src, mod8, norm2_g[i][None],
                                     w_out[i].astype(BF16), router2, tile_off, n_ctx_tiles)
        segs = []
        if need_ctx:
            segs.append((0, ctx_len, EC_CAPACITY * ctx_len // N_EXPERTS))
        segs.append((ctx_len - tok_off, lat_len, EC_CAPACITY * lat_len // N_EXPERTS))
        pos, aff, win0, pages = _topk(probs_t, tuple(segs))
        xg = _moe_gather(win0, pages, pos, h2, sum(k for _, _, k in segs))
        y = _moe_ffn(xg, exp_w_gate, exp_w_up, exp_w_down, i)
        out = _moe_combine(win0, pages, pos, aff, y, xmid, mod8, tile_off, n_ctx_tiles)
        ctx_src = lat_src = out
    return out
```

```python
import functools
import math

import jax
import jax.numpy as jnp
import numpy as np
from jax import lax
from jax.experimental import pallas as pl
from jax.experimental.pallas import tpu as pltpu

F32 = jnp.float32
BF16 = jnp.bfloat16
I32 = jnp.int32

ATT_HEADS = 4
ATT_QK = 64
ATT_V = 2 * ATT_QK
ATT_W = ATT_HEADS * ATT_V
RET_HEADS = 4
RET_QK = 64
RET_W = 256
LRU_W = 256
LRU_C = 8.0
N_EXPERTS = 16
EC_CAPACITY = 2
EPS = 1e-6
ROPE_BASE = 10000.0
GRID_W = 64
IN_COLS = 3072
Q_SCALE = ATT_QK ** -0.5 * math.log2(math.e)

LANES = 128
SUBLANES = 8
TOK_TILE = 256
KEY_BLOCK = 512
CHUNK = 256
RET_CHUNK = 256
TOK_BLOCK = 256
SLOT_WIN = 64
SLOT_ALIGN = 16
VMEM_LIMIT = 56 * 1024 * 1024


def _cparams(sem, vmem=None):
    return pltpu.CompilerParams(dimension_semantics=sem, vmem_limit_bytes=vmem)


def _sigmoid(x):
    return 0.5 * (jnp.tanh(0.5 * x) + 1.0)


def _silu(x):
    return x * _sigmoid(x)


def _group_mean_sq(t, bd):
    sq = t * t
    hi = sq.astype(BF16)
    lo = (sq - hi.astype(F32)).astype(BF16)
    ss = jnp.dot(hi, bd, preferred_element_type=F32) + jnp.dot(lo, bd, preferred_element_type=F32)
    return ss * (1.0 / 64.0)


def _mod_kernel(c_ref, w_ref, b_ref, o_ref):
    s = _silu(c_ref[...]).astype(BF16)
    o_ref[...] = jnp.dot(s, w_ref[...].astype(BF16), preferred_element_type=F32) + b_ref[...]


def _modulation(cc, mod_w, mod_b):
    depth, d, d6 = mod_w.shape
    rows = cc.shape[0]
    return pl.pallas_call(
        _mod_kernel,
        grid=(depth, d6 // d),
        in_specs=[pl.BlockSpec((rows, d), lambda l, j: (0, 0)),
                  pl.BlockSpec((None, d, d), lambda l, j: (l, 0, j)),
                  pl.BlockSpec((None, 1, d), lambda l, j: (l, 0, j))],
        out_specs=pl.BlockSpec((None, rows, d), lambda l, j: (l, 0, j)),
        out_shape=jax.ShapeDtypeStruct((depth, rows, d6), F32),
        compiler_params=_cparams(("arbitrary", "arbitrary")),
        name="modulation",
    )(cc, mod_w, mod_b.reshape(depth, 1, d6))


def _samples_per_step(b):
    return 2 if b % 2 == 0 else 1


def _token_sources(ctx_src, lat_src, tile_off, n_ctx_tiles, d, sb):
    lat_base = n_ctx_tiles if lat_src is ctx_src else 0
    ctx_spec = pl.BlockSpec((sb, TOK_TILE, d), lambda bi, i, *_: (bi, jnp.minimum(i + tile_off, n_ctx_tiles - 1), 0))
    lat_spec = pl.BlockSpec((sb, TOK_TILE, d),
                            lambda bi, i, *_: (bi, jnp.maximum(i + tile_off - n_ctx_tiles, 0) + lat_base, 0))
    return ctx_spec, lat_spec


def _put_rows(ref, val, cols=slice(None)):
    rows = ref.shape[1]
    for si in range(ref.shape[0]):
        ref[si, :, cols] = val[si * rows:(si + 1) * rows].astype(ref.dtype)


def _inproj_kernel(xc_ref, xl_ref, mod_ref, ng_ref, w_ref, rope_ref, gqk_ref, bd_ref,
                   q_ref, kt_ref, v_ref, rq_ref, rk_ref, rv_ref, gr_ref, xu_ref, gu_ref, *, n_ctx_tiles):
    sb, tm, _ = xc_ref.shape
    is_ctx = pl.program_id(1) < n_ctx_tiles
    hs = []
    for si in range(sb):
        x = jnp.where(is_ctx, xc_ref[si], xl_ref[si])
        ms = jnp.mean(x * x, axis=-1, keepdims=True)
        y = x * lax.rsqrt(ms + EPS) * ng_ref[...]
        hs.append((y * (1.0 + mod_ref[si, 1:2, :]) + mod_ref[si, 0:1, :]).astype(BF16))
    hb = jnp.concatenate(hs, axis=0)

    def proj(c0, c1):
        return jnp.dot(hb, w_ref[:, c0:c1], preferred_element_type=F32)

    bd = bd_ref[...]
    cosr = rope_ref[0]
    sin_up = rope_ref[1]
    sin_dn = rope_ref[2]

    def norm_rope(t, ms, g):
        tn = t * lax.rsqrt(ms + EPS) * g
        return tn * cosr + pltpu.roll(tn, LANES - 16, 1) * sin_up + pltpu.roll(tn, 16, 1) * sin_dn

    tqk = proj(0, 2 * ATT_W)
    for hd in range(ATT_HEADS):
        c0 = hd * ATT_V
        tq = tqk[:, c0:c0 + ATT_V]
        tk = tqk[:, ATT_W + c0:ATT_W + c0 + ATT_V]
        ms = _group_mean_sq(jnp.concatenate([tq, tk], axis=1), bd)
        for si in range(sb):
            rows = slice(si * tm, (si + 1) * tm)
            q_ref[si, :, c0:c0 + ATT_V] = (norm_rope(tq[rows], ms[rows, :ATT_V], gqk_ref[0:1, :])
                                           * Q_SCALE).astype(BF16)
            kt_ref[si, c0:c0 + ATT_V, :] = norm_rope(tk[rows], ms[rows, ATT_V:], gqk_ref[1:2, :]).T.astype(BF16)
    _put_rows(v_ref, proj(2 * ATT_W, 3 * ATT_W))
    base = 3 * ATT_W
    _put_rows(rq_ref, proj(base, base + RET_W))
    _put_rows(rk_ref, proj(base + RET_W, base + 2 * RET_W) * (RET_QK ** -0.5))
    _put_rows(rv_ref, proj(base + 2 * RET_W, base + 3 * RET_W))
    _put_rows(gr_ref, proj(base + 3 * RET_W, base + 4 * RET_W))
    _put_rows(xu_ref, proj(base + 4 * RET_W, base + 4 * RET_W + LRU_W))
    _put_rows(gu_ref, proj(base + 4 * RET_W + LRU_W, base + 4 * RET_W + 2 * LRU_W))


def _inproj(ctx_src, lat_src, s, mod8, norm_g, w_in_b, rope, gqk, bd, n_ctx_tiles):
    b, _, d = ctx_src.shape
    tm = TOK_TILE
    sb = _samples_per_step(b)
    seg = lambda i: jnp.minimum(i // n_ctx_tiles, 1)
    tok = lambda w: pl.BlockSpec((sb, tm, w), lambda bi, i: (bi, i, 0))
    f32o = lambda w: jax.ShapeDtypeStruct((b, s, w), F32)
    ctx_spec, lat_spec = _token_sources(ctx_src, lat_src, 0, n_ctx_tiles, d, sb)
    return pl.pallas_call(
        functools.partial(_inproj_kernel, n_ctx_tiles=n_ctx_tiles),
        grid=(b // sb, s // tm),
        in_specs=[ctx_spec, lat_spec,
                  pl.BlockSpec((sb, None, 8, d), lambda bi, i: (bi, seg(i), 0, 0)),
                  pl.BlockSpec((1, d), lambda bi, i: (0, 0)),
                  pl.BlockSpec((d, IN_COLS), lambda bi, i: (0, 0)),
                  pl.BlockSpec((3, tm, LANES), lambda bi, i: (0, i, 0)),
                  pl.BlockSpec((2, LANES), lambda bi, i: (0, 0)),
                  pl.BlockSpec((2 * LANES, 2 * LANES), lambda bi, i: (0, 0))],
        out_specs=[tok(ATT_W),
                   pl.BlockSpec((sb, ATT_W, tm), lambda bi, i: (bi, 0, i)),
                   tok(ATT_W), tok(RET_W), tok(RET_W), tok(RET_W), tok(RET_W), tok(LRU_W), tok(LRU_W)],
        out_shape=[jax.ShapeDtypeStruct((b, s, ATT_W), BF16),
                   jax.ShapeDtypeStruct((b, ATT_W, s), BF16),
                   jax.ShapeDtypeStruct((b, s, ATT_W), BF16),
                   f32o(RET_W), f32o(RET_W), f32o(RET_W), f32o(RET_W), f32o(LRU_W), f32o(LRU_W)],
        compiler_params=_cparams(("arbitrary", "arbitrary"), VMEM_LIMIT),
        name="inproj",
    )(ctx_src, lat_src, mod8, norm_g, w_in_b, rope, gqk, bd)


def _attn_kernel(lam_ref, q_ref, kt_ref, v_ref, g_ref, o_ref, *, ctx_len, q_off, n_ctx_tiles, lam_init):
    seq_len = kt_ref.shape[1]
    lp = lam_ref[...]
    lam = (jnp.exp(jnp.sum(lp[0:1, :] * lp[1:2, :], axis=1, keepdims=True))
           - jnp.exp(jnp.sum(lp[2:3, :] * lp[3:4, :], axis=1, keepdims=True)) + lam_init)
    q = q_ref[...]
    lane = lax.broadcasted_iota(I32, q.shape, 1)

    def attend(n_keys):
        outs = []
        for mi in range(2):
            keep = (lane < ATT_QK) if mi == 0 else (lane >= ATT_QK)
            qm = jnp.where(keep, q, jnp.zeros_like(q))
            s = jnp.dot(qm, kt_ref[:, 0:n_keys], preferred_element_type=F32)
            p = jnp.exp2(s - jnp.max(s, axis=1, keepdims=True))
            outs.append((p, jnp.sum(p, axis=1, keepdims=True)))
        a = outs[0][0] * (1.0 / outs[0][1]) - outs[1][0] * (lam / outs[1][1])
        o = jnp.dot(a.astype(BF16), v_ref[0:n_keys, :], preferred_element_type=F32)
        o = o * lax.rsqrt(jnp.mean(o * o, axis=-1, keepdims=True) + EPS) * g_ref[...] * (1.0 - lam_init)
        o_ref[...] = o.astype(BF16)

    if q_off >= n_ctx_tiles:
        attend(seq_len)
    else:
        is_ctx = pl.program_id(2) + q_off < n_ctx_tiles
        pl.when(is_ctx)(lambda: attend(ctx_len))
        pl.when(jnp.logical_not(is_ctx))(lambda: attend(seq_len))


def _attention(lam_p, q, kt, v, subln_g, ctx_len, q_off, n_ctx_tiles, lam_init):
    b, s, _ = q.shape
    tq = TOK_TILE
    nq = s // tq - q_off
    kern = functools.partial(_attn_kernel, ctx_len=ctx_len, q_off=q_off, n_ctx_tiles=n_ctx_tiles,
                             lam_init=lam_init)
    return pl.pallas_call(
        kern,
        grid=(b, ATT_HEADS, nq),
        in_specs=[pl.BlockSpec((4, LANES), lambda bi, h, i: (0, 0)),
                  pl.BlockSpec((None, tq, ATT_V), lambda bi, h, i: (bi, i + q_off, h)),
                  pl.BlockSpec((None, ATT_V, s), lambda bi, h, i: (bi, h, 0)),
                  pl.BlockSpec((None, s, ATT_V), lambda bi, h, i: (bi, 0, h)),
                  pl.BlockSpec((1, ATT_V), lambda bi, h, i: (0, 0))],
        out_specs=pl.BlockSpec((None, tq, ATT_V), lambda bi, h, i: (bi, i, h)),
        out_shape=jax.ShapeDtypeStruct((b, nq * tq, ATT_W), BF16),
        compiler_params=_cparams(("arbitrary", "arbitrary", "arbitrary"), VMEM_LIMIT),
        name="diff_attention",
    )(lam_p, q, kt, v, subln_g)


def _ret_kernel(lg_ref, q_ref, k_ref, v_ref, gr_ref, g_ref, bd_ref, o_ref,
                of_ref, ob_ref, st_ref, dm_ref, cr_ref, in_ref, cd_ref, *, n_ctx_chunks, n_chunks, out_chunk0):
    c = RET_CHUNK
    rown = lax.broadcasted_iota(I32, (c, c), 0)
    colm = lax.broadcasted_iota(I32, (c, c), 1)
    low = lax.broadcasted_iota(I32, (c, LANES), 1) < RET_QK
    rowf = lax.broadcasted_iota(I32, (c, LANES), 0).astype(F32)
    srow = lax.broadcasted_iota(I32, (LANES, LANES), 0)
    scol = lax.broadcasted_iota(I32, (LANES, LANES), 1)
    for d in range(2):
        if d == 0:
            diff = (rown - colm).astype(F32)
            mask = rown >= colm
            cross_pw = rowf + 1.0
            inner_pw = (c - 1.0) - rowf
        else:
            diff = (colm - rown).astype(F32)
            mask = colm > rown
            cross_pw = c - rowf
            inner_pw = rowf
        for hp in range(2):
            dm_ref[d, hp] = jnp.concatenate(
                [jnp.where(mask, jnp.exp(lg_ref[d, 2 * hp + j] * jnp.where(mask, diff, 0.0)), 0.0) for j in range(2)],
                axis=1)
            lg_lane = jnp.where(low, lg_ref[d, 2 * hp], lg_ref[d, 2 * hp + 1])
            cr_ref[d, hp] = jnp.exp(lg_lane * cross_pw)
            in_ref[d, hp] = jnp.exp(lg_lane * inner_pw)
            cd_ref[d, hp] = jnp.exp(jnp.where(srow < RET_QK, lg_ref[d, 2 * hp], lg_ref[d, 2 * hp + 1]) * float(c))
    st_ref[...] = jnp.zeros(st_ref.shape, F32)
    bdmask = (srow < RET_QK) == (scol < RET_QK)

    def step(i, carry):
        for d in range(2):
            if d == 0:
                ci = i
            else:
                ci = jnp.where(i < n_ctx_chunks, n_ctx_chunks - 1 - i, n_chunks - 1 - (i - n_ctx_chunks))
            r0 = pl.multiple_of(ci * c, c)
            dst = of_ref if d == 0 else ob_ref
            for hp in range(2):
                cols = slice(hp * LANES, (hp + 1) * LANES)
                kf = k_ref[pl.ds(r0, c), cols]
                qf = q_ref[pl.ds(r0, c), cols]
                kb = kf.astype(BF16)
                vb = v_ref[pl.ds(r0, c), cols].astype(BF16)
                zero = jnp.zeros_like(kb)
                nt = (((1,), (1,)), ((), ()))
                kcat = jnp.concatenate([jnp.where(low, kb, zero), jnp.where(low, zero, kb)], axis=0)
                sc = (lax.dot_general(qf.astype(BF16), kcat, nt, preferred_element_type=F32)
                      * dm_ref[d, hp]).astype(BF16)
                vbd = jnp.concatenate([jnp.where(low, vb, zero), jnp.where(low, zero, vb)], axis=0)
                st = st_ref[d, hp]
                intra = jnp.dot(sc, vbd, preferred_element_type=F32)
                inter = jnp.dot(qf.astype(BF16), st.astype(BF16), preferred_element_type=F32) * cr_ref[d, hp]
                dst[pl.ds(r0, c), cols] = intra + inter
                kin = (kf * in_ref[d, hp]).astype(BF16)
                kv = lax.dot_general(kin, vb, (((0,), (0,)), ((), ())), preferred_element_type=F32)
                st_ref[d, hp] = st * cd_ref[d, hp] + jnp.where(bdmask, kv, 0.0)
        return carry

    lax.fori_loop(0, n_chunks, step, 0, unroll=2)

    bd = bd_ref[...]

    def finish(i, carry):
        r0 = pl.multiple_of((i + out_chunk0) * c, c)
        ro = pl.multiple_of(i * c, c)
        o = of_ref[pl.ds(r0, c), :] + ob_ref[pl.ds(r0, c), :]
        on = o * lax.rsqrt(_group_mean_sq(o, bd) + EPS) * g_ref[...]
        o_ref[pl.ds(ro, c), :] = (on * _silu(gr_ref[pl.ds(r0, c), :])).astype(BF16)
        return carry

    lax.fori_loop(0, n_chunks - out_chunk0, finish, 0)


def _retention(log_decay, rq, rk, rv, gr, norm_g, bd, ctx_len, out_off):
    b, s, w = rq.shape
    c = RET_CHUNK
    assert ctx_len % c == 0 and s % c == 0 and out_off % c == 0 and w == 2 * LANES
    n_chunks = s // c
    out_chunk0 = out_off // c
    kern = functools.partial(_ret_kernel, n_ctx_chunks=ctx_len // c, n_chunks=n_chunks, out_chunk0=out_chunk0)
    seq = pl.BlockSpec((None, s, w), lambda bi: (bi, 0, 0))
    state = pltpu.VMEM((2, 2, LANES, LANES), F32)
    rows = pltpu.VMEM((2, 2, c, LANES), F32)
    return pl.pallas_call(
        kern,
        grid=(b,),
        in_specs=[pl.BlockSpec(memory_space=pltpu.SMEM), seq, seq, seq, seq,
                  pl.BlockSpec((1, w), lambda bi: (0, 0)),
                  pl.BlockSpec((2 * LANES, 2 * LANES), lambda bi: (0, 0))],
        out_specs=pl.BlockSpec((None, s - out_off, w), lambda bi: (bi, 0, 0)),
        out_shape=jax.ShapeDtypeStruct((b, s - out_off, w), BF16),
        scratch_shapes=[pltpu.VMEM((s, w), F32), pltpu.VMEM((s, w), F32), state,
                        pltpu.VMEM((2, 2, c, 2 * c), F32), rows, rows, state],
        compiler_params=_cparams(("arbitrary",), VMEM_LIMIT),
        name="retention",
    )(log_decay, rq, rk, rv, gr, norm_g, bd)


def _gelu_tanh(x):
    return 0.5 * x * (1.0 + jnp.tanh(math.sqrt(2.0 / math.pi) * (x + 0.044715 * (x * x * x))))


def _lru_kernel(xu_ref, gu_ref, cw_ref, cb_ref, gw_ref, gb_ref, lam_ref, ng_ref, o_ref,
                u_ref, hf_ref, hb_ref, *, ctx_len, seq_len, out_chunk0):
    c = CHUNK
    n_chunks = seq_len // c
    n_ctx_chunks = ctx_len // c
    row = lax.broadcasted_iota(I32, (c, LRU_W), 0)

    def conv(i, carry):
        r0 = pl.multiple_of(i * c, c)
        prev0 = pl.multiple_of(jnp.maximum(r0 - 8, 0), 8)
        next0 = pl.multiple_of(jnp.minimum(r0 + c, seq_len - 8), 8)
        ext = jnp.concatenate([xu_ref[pl.ds(prev0, 8), :], xu_ref[pl.ds(r0, c), :], xu_ref[pl.ds(next0, 8), :]],
                              axis=0)
        seg_first = jnp.logical_or(i == 0, i == n_ctx_chunks)
        seg_last = jnp.logical_or(i == n_ctx_chunks - 1, i == n_chunks - 1)
        xm2 = pltpu.roll(ext, 2, 0)[8:8 + c]
        xm1 = pltpu.roll(ext, 1, 0)[8:8 + c]
        xp1 = pltpu.roll(ext, c + 16 - 1, 0)[8:8 + c]
        xm2 = jnp.where(jnp.logical_and(seg_first, row < 2), 0.0, xm2)
        xm1 = jnp.where(jnp.logical_and(seg_first, row < 1), 0.0, xm1)
        xp1 = jnp.where(jnp.logical_and(seg_last, row >= c - 1), 0.0, xp1)
        u_ref[pl.ds(r0, c), :] = (cw_ref[0:1, :] * xm2 + cw_ref[1:2, :] * xm1 + cw_ref[2:3, :] * ext[8:8 + c]
                                  + cw_ref[3:4, :] * xp1 + cb_ref[...])
        return carry

    lax.fori_loop(0, n_chunks, conv, 0)

    def softplus(z):
        return jnp.maximum(z, 0.0) + jnp.log1p(jnp.exp(-jnp.abs(z)))

    def scan_dir(d, u):
        ub = u.astype(BF16)
        zr = jnp.dot(ub, gw_ref[d, 0], preferred_element_type=F32) + gb_ref[2 * d:2 * d + 1, :]
        zi = jnp.dot(ub, gw_ref[d, 1], preferred_element_type=F32) + gb_ref[2 * d + 1:2 * d + 2, :]
        log_a = -LRU_C * softplus(-lam_ref[d:d + 1, :]) * _sigmoid(zr)
        a = jnp.exp(log_a)
        b = jnp.sqrt(-jnp.tanh(log_a) * (a * a + 1.0)) * (_sigmoid(zi) * u)
        ng = c // SUBLANES
        a = a.reshape(ng, SUBLANES, LRU_W)
        b = b.reshape(ng, SUBLANES, LRU_W)
        sub = lax.broadcasted_iota(I32, (ng, SUBLANES, LRU_W), 1)
        sh = 1
        while sh < SUBLANES:
            if d == 0:
                valid = sub >= sh
                a_prev = pltpu.roll(a, sh, 1)
                b_prev = pltpu.roll(b, sh, 1)
            else:
                valid = sub < SUBLANES - sh
                a_prev = pltpu.roll(a, SUBLANES - sh, 1)
                b_prev = pltpu.roll(b, SUBLANES - sh, 1)
            b = a * jnp.where(valid, b_prev, 0.0) + b
            a = a * jnp.where(valid, a_prev, 1.0)
            sh *= 2
        return a, b

    def chain(d, a, b, h):
        ng = c // SUBLANES
        out = [None] * ng
        for g in (range(ng) if d == 0 else range(ng - 1, -1, -1)):
            hg = a[g] * h + b[g]
            out[g] = hg
            h = hg[SUBLANES - 1:SUBLANES, :] if d == 0 else hg[0:1, :]
        return jnp.concatenate(out, axis=0), h

    def step(i, carry):
        h_f, h_b = carry
        r0 = pl.multiple_of(i * c, c)
        a, b = scan_dir(0, u_ref[pl.ds(r0, c), :])
        hh, h_f = chain(0, a, b, h_f)
        hf_ref[pl.ds(r0, c), :] = hh
        cb = jnp.where(i < n_ctx_chunks, n_ctx_chunks - 1 - i, n_chunks - 1 - (i - n_ctx_chunks))
        rb = pl.multiple_of(cb * c, c)
        a, b = scan_dir(1, u_ref[pl.ds(rb, c), :])
        hh, h_b = chain(1, a, b, h_b)
        hb_ref[pl.ds(rb, c), :] = hh
        return h_f, h_b

    zero = jnp.zeros((1, LRU_W), F32)
    lax.fori_loop(0, n_chunks, step, (zero, zero))

    def finish(i, carry):
        r0 = pl.multiple_of((i + out_chunk0) * c, c)
        ro = pl.multiple_of(i * c, c)
        t = (hf_ref[pl.ds(r0, c), :] + hb_ref[pl.ds(r0, c), :]) * _gelu_tanh(gu_ref[pl.ds(r0, c), :])
        t = t * lax.rsqrt(jnp.mean(t * t, axis=-1, keepdims=True) + EPS) * ng_ref[...]
        o_ref[pl.ds(ro, c), :] = t.astype(BF16)
        return carry

    lax.fori_loop(0, n_chunks - out_chunk0, finish, 0)


def _rglru(xu, gu, conv_w, conv_b, gate_wd, gate_b, lam, norm_g, ctx_len, out_off):
    b, s, w = xu.shape
    kern = functools.partial(_lru_kernel, ctx_len=ctx_len, seq_len=s, out_chunk0=out_off // CHUNK)
    seq = pl.BlockSpec((None, s, w), lambda bi: (bi, 0, 0))
    full = lambda shp: pl.BlockSpec(shp, lambda bi: (0,) * len(shp))
    return pl.pallas_call(
        kern,
        grid=(b,),
        in_specs=[seq, seq, full((4, w)), full((1, w)), full((2, 2, w, w)), full((4, w)), full((2, w)),
                  full((1, w))],
        out_specs=pl.BlockSpec((None, s - out_off, w), lambda bi: (bi, 0, 0)),
        out_shape=jax.ShapeDtypeStruct((b, s - out_off, w), BF16),
        scratch_shapes=[pltpu.VMEM((s, w), F32), pltpu.VMEM((s, w), F32), pltpu.VMEM((s, w), F32)],
        compiler_params=_cparams(("arbitrary",), VMEM_LIMIT),
        name="rglru",
    )(xu, gu, conv_w, conv_b, gate_wd, gate_b, lam, norm_g)


def _outproj_kernel(att_ref, ret_ref, lru_ref, xc_ref, xl_ref, mod_ref, ng_ref, w_ref, rw_ref,
                    xmid_ref, h2_ref, pt_ref, *, tile_off, n_ctx_tiles):
    sb, tm, _ = att_ref.shape
    mix = jnp.concatenate([jnp.concatenate([att_ref[si], ret_ref[si], lru_ref[si]], axis=1) for si in range(sb)],
                          axis=0)
    y = jnp.dot(mix, w_ref[...], preferred_element_type=F32)
    is_ctx = pl.program_id(1) + tile_off < n_ctx_tiles
    hs = []
    for si in range(sb):
        x = jnp.where(is_ctx, xc_ref[si], xl_ref[si]) + mod_ref[si, 2:3, :] * y[si * tm:(si + 1) * tm]
        xmid_ref[si] = x
        ms = jnp.mean(x * x, axis=-1, keepdims=True)
        hs.append((x * lax.rsqrt(ms + EPS) * ng_ref[...]) * (1.0 + mod_ref[si, 4:5, :]) + mod_ref[si, 3:4, :])
    h = jnp.concatenate(hs, axis=0)
    hb = h.astype(BF16)
    _put_rows(h2_ref, hb)
    hlo = (h - hb.astype(F32)).astype(BF16)
    wide = jnp.dot(hb, rw_ref[...], preferred_element_type=F32)
    logits = (wide[:, :LANES] + wide[:, LANES:]
              + jnp.dot(hlo, rw_ref[:, :LANES], preferred_element_type=F32))
    lane = lax.broadcasted_iota(I32, logits.shape, 1)
    logits = jnp.where(lane < N_EXPERTS, logits, -jnp.inf)
    e = jnp.exp(logits - jnp.max(logits, axis=-1, keepdims=True))
    probs = e / jnp.sum(e, axis=-1, keepdims=True)
    for si in range(sb):
        pt_ref[si] = probs[si * tm:(si + 1) * tm].T[0:N_EXPERTS, :]


def _outproj(att, ret, lru, ctx_src, lat_src, mod8, norm_g, w_out_b, router2, tile_off, n_ctx_tiles):
    b, so, _ = att.shape
    d = ctx_src.shape[-1]
    tm = TOK_TILE
    sb = 1
    seg = lambda i: jnp.minimum((i + tile_off) // n_ctx_tiles, 1)
    tok = lambda w: pl.BlockSpec((sb, tm, w), lambda bi, i: (bi, i, 0))
    ctx_spec, lat_spec = _token_sources(ctx_src, lat_src, tile_off, n_ctx_tiles, d, sb)
    return pl.pallas_call(
        functools.partial(_outproj_kernel, tile_off=tile_off, n_ctx_tiles=n_ctx_tiles),
        grid=(b // sb, so // tm),
        in_specs=[tok(ATT_W), tok(RET_W), tok(LRU_W), ctx_spec, lat_spec,
                  pl.BlockSpec((sb, None, 8, d), lambda bi, i: (bi, seg(i), 0, 0)),
                  pl.BlockSpec((1, d), lambda bi, i: (0, 0)),
                  pl.BlockSpec((d, d), lambda bi, i: (0, 0)),
                  pl.BlockSpec((d, 2 * LANES), lambda bi, i: (0, 0))],
        out_specs=[tok(d), tok(d), pl.BlockSpec((sb, N_EXPERTS, tm), lambda bi, i: (bi, 0, i))],
        out_shape=[jax.ShapeDtypeStruct((b, so, d), F32), jax.ShapeDtypeStruct((b, so, d), BF16),
                   jax.ShapeDtypeStruct((b, N_EXPERTS, so), F32)],
        compiler_params=_cparams(("arbitrary", "arbitrary"), VMEM_LIMIT),
        name="outproj_router",
    )(att, ret, lru, ctx_src, lat_src, mod8, norm_g, w_out_b, router2)


def _topk_kernel(p_ref, pos_ref, aff_ref, win_ref, pages_ref, *, segs):
    ne = p_ref.shape[0]
    block_first_slot = []
    ri = lax.broadcasted_iota(I32, (LANES, LANES), 0)
    ci = lax.broadcasted_iota(I32, (LANES, LANES), 1)
    before = jnp.where(ri < ci, 1.0, 0.0).astype(BF16)
    run_sel = jnp.zeros((ne, 1), F32)

    def kth_floor(vals, k):
        t = jnp.zeros((ne, 1), I32)
        for bit in range(30, -1, -1):
            cand = t | (1 << bit)
            cnt = jnp.sum(jnp.where(vals >= pltpu.bitcast(cand, F32), 1.0, 0.0), axis=1, keepdims=True)
            t = jnp.where(cnt >= k, cand, t)
        return pltpu.bitcast(t, F32)

    for start, ln, k in segs:
        p = p_ref[:, start:start + ln]
        resid = p - kth_floor(p, k)
        t = kth_floor(resid, k)
        gt = resid > t
        eq = resid == t
        need = k - jnp.sum(jnp.where(gt, 1.0, 0.0), axis=1, keepdims=True)
        run_eq = jnp.zeros((ne, 1), F32)
        for j in range(ln // LANES):
            sl = slice(j * LANES, (j + 1) * LANES)
            if (start + j * LANES) % TOK_BLOCK == 0:
                block_first_slot.append(run_sel)
            eq_t = jnp.where(eq[:, sl], 1.0, 0.0)
            rank_eq = jnp.dot(eq_t.astype(BF16), before, preferred_element_type=F32) + run_eq
            sel = jnp.logical_or(gt[:, sl], jnp.logical_and(eq[:, sl], rank_eq < need))
            sel_t = jnp.where(sel, 1.0, 0.0)
            slot = jnp.dot(sel_t.astype(BF16), before, preferred_element_type=F32) + run_sel
            pos_ref[:, start + j * LANES:start + (j + 1) * LANES] = jnp.where(sel, slot, -1.0).astype(I32)
            aff_ref[:, start + j * LANES:start + (j + 1) * LANES] = jnp.where(sel, p[:, sl], 0.0)
            run_eq = run_eq + jnp.sum(eq_t, axis=1, keepdims=True)
            run_sel = run_sel + jnp.sum(sel_t, axis=1, keepdims=True)

    block_first_slot.append(run_sel)
    lane = lax.broadcasted_iota(I32, (ne, LANES), 1)
    win0 = jnp.zeros((ne, LANES), I32)
    pages = jnp.zeros((ne, LANES), I32)
    for j in range(len(block_first_slot) - 1):
        first = block_first_slot[j].astype(I32)
        end = block_first_slot[j + 1].astype(I32)
        w0 = jnp.left_shift(jnp.right_shift(first, SLOT_ALIGN.bit_length() - 1), SLOT_ALIGN.bit_length() - 1)
        need_pages = jnp.right_shift(end - w0 + (SLOT_WIN - 1), SLOT_WIN.bit_length() - 1)
        per_sample = jnp.max(need_pages.astype(F32).reshape(ne // N_EXPERTS, N_EXPERTS, 1), axis=1, keepdims=True)
        need_pages = jnp.broadcast_to(per_sample, (ne // N_EXPERTS, N_EXPERTS, 1)).reshape(ne, 1).astype(I32)
        win0 = jnp.where(lane == j, w0, win0)
        pages = jnp.where(lane == j, need_pages, pages)
    win_ref[...] = win0
    pages_ref[...] = pages


def _topk(probs_t, segs):
    b, ne, so = probs_t.shape
    assert so // TOK_BLOCK <= LANES and ne == N_EXPERTS
    rows = b * ne
    blk = pl.BlockSpec((rows, so), lambda i: (0, 0))
    tab = pl.BlockSpec((rows, LANES), lambda i: (0, 0))
    pos, aff, win0, pages = pl.pallas_call(
        functools.partial(_topk_kernel, segs=segs),
        grid=(1,),
        in_specs=[blk],
        out_specs=[blk, blk, tab, tab],
        out_shape=[jax.ShapeDtypeStruct((rows, so), I32), jax.ShapeDtypeStruct((rows, so), F32),
                   jax.ShapeDtypeStruct((rows, LANES), I32), jax.ShapeDtypeStruct((rows, LANES), I32)],
        compiler_params=_cparams(("arbitrary",), VMEM_LIMIT),
        name="expert_choice_topk",
    )(probs_t.reshape(rows, so))
    pos, aff = pos.reshape(b, ne, so), aff.reshape(b, ne, so)
    win0, pages = win0.reshape(b, ne, LANES), pages.reshape(b, ne, LANES)
    nblk = so // TOK_BLOCK
    win0 = jnp.swapaxes(win0[:, :, :nblk], 1, 2).reshape(-1)
    pages = pages[:, 0, :nblk].reshape(-1)
    return pos, aff, win0, pages


def _slot_windows(win_ref, base, page, n_slots):
    wins = []
    for e in range(N_EXPERTS):
        lo = win_ref[base + e] + page * SLOT_WIN
        wins.append((lo, pl.multiple_of(jnp.minimum(lo, n_slots - SLOT_WIN), SLOT_ALIGN)))
    return wins


def _window_hits(pos, wins):
    row = lax.broadcasted_iota(I32, (SLOT_WIN, TOK_BLOCK), 0)
    hits = []
    for e, (lo, start) in enumerate(wins):
        pe = pos[e:e + 1, :]
        hits.append(jnp.logical_and(pe - start == row, pe >= lo))
    return hits


def _moe_gather_kernel(win_ref, pages_ref, pos_ref, h_ref, xg_ref, *, n_slots):
    j = pl.program_id(1)
    blk = pl.program_id(0) * pl.num_programs(1) + j

    @pl.when(j == 0)
    def _():
        xg_ref[...] = jnp.zeros(xg_ref.shape, BF16)

    pos = pos_ref[...]

    def page(r, carry):
        wins = _slot_windows(win_ref, blk * N_EXPERTS, r, n_slots)
        onehot = jnp.concatenate([jnp.where(hit, 1.0, 0.0).astype(BF16) for hit in _window_hits(pos, wins)], axis=0)
        g = jnp.dot(onehot, h_ref[...], preferred_element_type=F32)
        for e, (_, start) in enumerate(wins):
            xg_ref[e, pl.ds(start, SLOT_WIN), :] += g[e * SLOT_WIN:(e + 1) * SLOT_WIN, :].astype(BF16)
        return carry

    lax.fori_loop(0, pages_ref[blk], page, 0)


def _moe_gather(win0, pages, pos, h2, n_slots):
    b, so, d = h2.shape
    ne = pos.shape[1]
    assert so % TOK_BLOCK == 0 and n_slots >= SLOT_WIN and n_slots % SLOT_ALIGN == 0 and ne == N_EXPERTS
    grid_spec = pltpu.PrefetchScalarGridSpec(
        num_scalar_prefetch=2,
        grid=(b, so // TOK_BLOCK),
        in_specs=[pl.BlockSpec((None, ne, TOK_BLOCK), lambda bi, j, w, p: (bi, 0, j)),
                  pl.BlockSpec((None, TOK_BLOCK, d), lambda bi, j, w, p: (bi, j, 0))],
        out_specs=pl.BlockSpec((None, ne, n_slots, d), lambda bi, j, w, p: (bi, 0, 0, 0)))
    return pl.pallas_call(
        functools.partial(_moe_gather_kernel, n_slots=n_slots),
        grid_spec=grid_spec,
        out_shape=jax.ShapeDtypeStruct((b, ne, n_slots, d), BF16),
        compiler_params=_cparams(("arbitrary", "arbitrary"), VMEM_LIMIT),
        name="moe_gather",
    )(win0, pages, pos, h2)


def _moe_ffn_kernel(x_ref, wg_ref, wu_ref, wd_ref, y_ref, wgb_ref, wub_ref, wdb_ref):
    @pl.when(pl.program_id(1) == 0)
    def _():
        wgb_ref[...] = wg_ref[...].astype(BF16)
        wub_ref[...] = wu_ref[...].astype(BF16)
        wdb_ref[...] = wd_ref[...].astype(BF16)

    xb = jnp.concatenate([x_ref[si] for si in range(x_ref.shape[0])], axis=0)
    gate = jnp.dot(xb, wgb_ref[...], preferred_element_type=F32)
    up = jnp.dot(xb, wub_ref[...], preferred_element_type=F32)
    hid = (_silu(gate) * up).astype(BF16)
    _put_rows(y_ref, jnp.dot(hid, wdb_ref[...], preferred_element_type=F32))


def _moe_ffn(xg, wg, wu, wd, layer):
    b, ne, n_slots, d = xg.shape
    ff = wg.shape[3]
    sb = _samples_per_step(b)
    rows = pl.BlockSpec((sb, None, n_slots, d), lambda e, bi: (bi, e, 0, 0))
    return pl.pallas_call(
        _moe_ffn_kernel,
        grid=(ne, b // sb),
        in_specs=[rows,
                  pl.BlockSpec((None, None, d, ff), lambda e, bi: (layer, e, 0, 0)),
                  pl.BlockSpec((None, None, d, ff), lambda e, bi: (layer, e, 0, 0)),
                  pl.BlockSpec((None, None, ff, d), lambda e, bi: (layer, e, 0, 0))],
        out_specs=rows,
        out_shape=jax.ShapeDtypeStruct((b, ne, n_slots, d), BF16),
        scratch_shapes=[pltpu.VMEM((d, ff), BF16), pltpu.VMEM((d, ff), BF16), pltpu.VMEM((ff, d), BF16)],
        compiler_params=_cparams(("arbitrary", "arbitrary"), VMEM_LIMIT),
        name="expert_ffn",
    )(xg, wg, wu, wd)


def _moe_combine_kernel(win_ref, pages_ref, pos_ref, aff_ref, y_ref, x_ref, mod_ref, o_ref, ycat_ref, acc_ref, *,
                        n_slots):
    blk = pl.program_id(0) * pl.num_programs(1) + pl.program_id(1)
    pos = pos_ref[...]
    aff = aff_ref[...]
    aff_hi = aff.astype(BF16).astype(F32)
    aff_lo = aff - aff_hi
    acc_ref[...] = jnp.zeros(acc_ref.shape, F32)
    tn = (((0,), (0,)), ((), ()))

    def page(r, carry):
        wins = _slot_windows(win_ref, blk * N_EXPERTS, r, n_slots)
        hits = _window_hits(pos, wins)
        for e, (_, start) in enumerate(wins):
            ycat_ref[e * SLOT_WIN:(e + 1) * SLOT_WIN, :] = y_ref[e, pl.ds(start, SLOT_WIN), :]
        w_hi = jnp.concatenate([jnp.where(hit, aff_hi[e:e + 1, :], 0.0).astype(BF16) for e, hit in enumerate(hits)],
                               axis=0)
        w_lo = jnp.concatenate([jnp.where(hit, aff_lo[e:e + 1, :], 0.0).astype(BF16) for e, hit in enumerate(hits)],
                               axis=0)
        ycat = ycat_ref[...]
        acc_ref[...] += (lax.dot_general(w_hi, ycat, tn, preferred_element_type=F32)
                         + lax.dot_general(w_lo, ycat, tn, preferred_element_type=F32))
        return carry

    lax.fori_loop(0, pages_ref[blk], page, 0)
    o_ref[...] = x_ref[...] + mod_ref[5:6, :] * acc_ref[...]


def _moe_combine(win0, pages, pos, aff, y, xmid, mod8, tile_off, n_ctx_tiles):
    b, so, d = xmid.shape
    _, ne, n_slots, _ = y.shape
    assert TOK_BLOCK == TOK_TILE
    if n_ctx_tiles and tile_off < n_ctx_tiles:
        seg = lambda i: jnp.minimum((i + tile_off) // n_ctx_tiles, 1)
    else:
        seg = lambda i: 1
    sel = pl.BlockSpec((None, ne, TOK_BLOCK), lambda bi, j, w, p: (bi, 0, j))
    tok = pl.BlockSpec((None, TOK_BLOCK, d), lambda bi, j, w, p: (bi, j, 0))
    grid_spec = pltpu.PrefetchScalarGridSpec(
        num_scalar_prefetch=2,
        grid=(b, so // TOK_BLOCK),
        in_specs=[sel, sel,
                  pl.BlockSpec((None, ne, n_slots, d), lambda bi, j, w, p: (bi, 0, 0, 0)),
                  tok,
                  pl.BlockSpec((None, None, 8, d), lambda bi, j, w, p: (bi, seg(j), 0, 0))],
        out_specs=tok,
        scratch_shapes=[pltpu.VMEM((ne * SLOT_WIN, d), BF16), pltpu.VMEM((TOK_BLOCK, d), F32)])
    return pl.pallas_call(
        functools.partial(_moe_combine_kernel, n_slots=n_slots),
        grid_spec=grid_spec,
        out_shape=jax.ShapeDtypeStruct((b, so, d), F32),
        compiler_params=_cparams(("arbitrary", "arbitrary"), VMEM_LIMIT),
        name="moe_combine",
    )(win0, pages, pos, aff, y, xmid, mod8)


def _rope_tables(ctx_len, lat_len):
    rows = lat_len // GRID_W
    row = jnp.repeat(jnp.arange(rows), GRID_W).astype(F32)
    col = jnp.tile(jnp.arange(GRID_W), rows).astype(F32)
    axis_dim = ATT_QK // 2
    inv = 1.0 / (ROPE_BASE ** (jnp.arange(0, axis_dim, 2, dtype=F32) / axis_dim))
    ang_row = row[:, None] * inv
    ang_col = col[:, None] * inv
    z = jnp.zeros_like(ang_row)
    cos64 = jnp.concatenate([jnp.cos(ang_row), jnp.cos(ang_row), jnp.cos(ang_col), jnp.cos(ang_col)], axis=-1)
    up64 = jnp.concatenate([-jnp.sin(ang_row), z, -jnp.sin(ang_col), z], axis=-1)
    dn64 = jnp.concatenate([z, jnp.sin(ang_row), z, jnp.sin(ang_col)], axis=-1)
    lat = jnp.stack([jnp.tile(t, (1, 2)) for t in (cos64, up64, dn64)])
    ctx = jnp.stack([jnp.ones((ctx_len, LANES), F32), jnp.zeros((ctx_len, LANES), F32),
                     jnp.zeros((ctx_len, LANES), F32)])
    return jnp.concatenate([ctx, lat], axis=1)


def kernel(x, c, ctx, c_ctx, mod_w, mod_b, norm1_g, norm2_g, w_in, w_out, att_q_norm_g, att_k_norm_g, att_lambda,
           att_subln_g, ret_log_decay, ret_norm_g, lru_conv_w, lru_conv_b, lru_gate_w, lru_gate_b, lru_lambda,
           lru_norm_g, router_w, exp_w_gate, exp_w_up, exp_w_down):
    b, lat_len, d = x.shape
    ctx_len = ctx.shape[1]
    depth = mod_w.shape[0]
    s = ctx_len + lat_len
    assert ctx_len % TOK_TILE == 0 and lat_len % KEY_BLOCK == 0 and lat_len % GRID_W == 0
    n_ctx_tiles = ctx_len // TOK_TILE

    rows = -(-(b + 1) // 8) * 8
    cc = jnp.zeros((rows, d), F32).at[:b].set(c).at[b].set(c_ctx)
    mod_all = _modulation(cc, mod_w, mod_b).reshape(depth, rows, 6, d)

    rope = _rope_tables(ctx_len, lat_len)
    lane = np.arange(2 * LANES)
    bd = jnp.asarray((lane[:, None] // 64) == (lane[None, :] // 64), BF16)
    ctx_src, lat_src = ctx, x

    out = None
    for i in range(depth):
        need_ctx = i < depth - 1
        m6 = mod_all[i]
        mod8 = jnp.stack([jnp.broadcast_to(m6[b], (b, 6, d)), m6[:b]], axis=1)
        mod8 = jnp.pad(mod8, ((0, 0), (0, 0), (0, 2), (0, 0)))
        gqk = jnp.stack([jnp.tile(att_q_norm_g[i], 2), jnp.tile(att_k_norm_g[i], 2)])
        q, kt, v, rq, rk, rv, gr, xu, gu = _inproj(ctx_src, lat_src, s, mod8, norm1_g[i][None],
                                                   w_in[i].astype(BF16), rope, gqk, bd, n_ctx_tiles)
        tile_off = 0 if need_ctx else n_ctx_tiles
        tok_off = tile_off * TOK_TILE
        lam_init = 0.8 - 0.6 * math.exp(-0.3 * i)
        lam_p = jnp.pad(att_lambda[i], ((0, 0), (0, LANES - ATT_QK)))
        att = _attention(lam_p, q, kt, v, att_subln_g[i][None], ctx_len, tile_off, n_ctx_tiles, lam_init)
        ret = _retention(ret_log_decay[i], rq, rk, rv, gr, jnp.tile(ret_norm_g[i], RET_HEADS)[None], bd, ctx_len,
                         tok_off)
        gw = lru_gate_w[i]
        gate_wd = jnp.zeros((2, 2, LRU_W, LRU_W), F32)
        bw = gw.shape[-1]
        for n in range(gw.shape[2]):
            gate_wd = gate_wd.at[:, :, n * bw:(n + 1) * bw, n * bw:(n + 1) * bw].set(gw[:, :, n])
        lru = _rglru(xu, gu, lru_conv_w[i], lru_conv_b[i][None], gate_wd.astype(BF16),
                     lru_gate_b[i].reshape(4, LRU_W), lru_lambda[i], lru_norm_g[i][None], ctx_len, tok_off)
        rw = jnp.pad(router_w[i], ((0, 0), (0, LANES - N_EXPERTS)))
        rw_hi = rw.astype(BF16)
        router2 = jnp.concatenate([rw_hi, (rw - rw_hi.astype(F32)).astype(BF16)], axis=1)
        xmid, h2, probs_t = _outproj(att, ret, lru, ctx_src, lat_src, mod8, norm2_g[i][None],
                                     w_out[i].astype(BF16), router2, tile_off, n_ctx_tiles)
        segs = []
        if need_ctx:
            segs.append((0, ctx_len, EC_CAPACITY * ctx_len // N_EXPERTS))
        segs.append((ctx_len - tok_off, lat_len, EC_CAPACITY * lat_len // N_EXPERTS))
        pos, aff, win0, pages = _topk(probs_t, tuple(segs))
        xg = _moe_gather(win0, pages, pos, h2, sum(k for _, _, k in segs))
        y = _moe_ffn(xg, exp_w_gate, exp_w_up, exp_w_down, i)
        out = _moe_combine(win0, pages, pos, aff, y, xmid, mod8, tile_off, n_ctx_tiles)
        ctx_src = lat_src = out
    return out
```
